```python
import math
import jax, jax.numpy as jnp
from jax import lax
import numpy as np

D_MODEL = 2048
BATCH = 16
SEQ = 2048
DEPTH = 2
DEC_BATCH = 4
DEC_SEQ = 8192
PAST_LEN = 128

D_INNER = 3072
SSD_HEAD_DIM = 64
SSD_HEADS = D_INNER // SSD_HEAD_DIM
SSD_GROUPS = 8
HEADS_PER_GROUP = SSD_HEADS // SSD_GROUPS
D_STATE = 128
CONV_K = 5
CONV_PAD = CONV_K // 2
CONV_DIM = D_INNER + 2 * SSD_GROUPS * D_STATE
CHUNK = 128
D_FOUR = 1024
FOUR_GROUPS = 8
FOUR_GROUP_DIM = D_FOUR // FOUR_GROUPS
D_FF = 5632
N_EXPERTS = 8
TOP_K = 2
ALPHA = (2 * DEPTH) ** 0.25
BETA = (8 * DEPTH) ** -0.25
LN_EPS = 1e-5
RMS_EPS = 1e-5
SPLITS = (
    D_INNER,
    D_INNER + CONV_DIM,
    D_INNER + CONV_DIM + SSD_HEADS,
    D_INNER + CONV_DIM + 2 * SSD_HEADS,
    D_INNER + CONV_DIM + 2 * SSD_HEADS + D_FOUR,
)
W_IN_COLS = D_INNER + CONV_DIM + 2 * SSD_HEADS + D_FOUR + 2 * D_MODEL

kernel_name = 'hybrid_ssd_fnet_deepnorm_encoder'


def layer_norm(x, g, b):
    xf = x.astype(jnp.float32)
    mu = jnp.mean(xf, axis=-1, keepdims=True)
    var = jnp.mean(jnp.square(xf - mu), axis=-1, keepdims=True)
    y = (xf - mu) * lax.rsqrt(var + LN_EPS) * g.astype(jnp.float32) + b.astype(jnp.float32)
    return y.astype(x.dtype)


def gated_group_rmsnorm(y, z, w):
    bsz, L, _ = y.shape
    v = (y.astype(jnp.float32) * jax.nn.silu(z.astype(jnp.float32)))
    v = v.reshape(bsz, L, SSD_GROUPS, D_INNER // SSD_GROUPS)
    v = v * lax.rsqrt(jnp.mean(jnp.square(v), axis=-1, keepdims=True) + RMS_EPS)
    return v.reshape(bsz, L, D_INNER) * w.astype(jnp.float32)


def ssd_chunked(xh, dt, a, bm, cm):
    bsz, L, G, H, P = xh.shape
    nc = L // CHUNK
    X = (xh * dt[..., None]).reshape(bsz, nc, CHUNK, G, H, P)
    A = (dt * a).reshape(bsz, nc, CHUNK, G, H)
    Bc = bm.reshape(bsz, nc, CHUNK, G, D_STATE)
    Cc = cm.reshape(bsz, nc, CHUNK, G, D_STATE)
    a_cum = jnp.cumsum(A, axis=2)
    mask = jnp.tril(jnp.ones((CHUNK, CHUNK), dtype=bool))[:, :, None, None]
    seg = a_cum[:, :, :, None] - a_cum[:, :, None, :]
    decay = jnp.exp(jnp.where(mask, seg, -jnp.inf))
    cb = jnp.einsum('bctgn,bcsgn->bctsg', Cc, Bc)
    y_diag = jnp.einsum('bctsgh,bcsghp->bctghp', cb[..., None] * decay, X)
    decay_states = jnp.exp(a_cum[:, :, -1:] - a_cum)
    states = jnp.einsum('bclgn,bclgh,bclghp->bcghpn', Bc, decay_states, X)
    chunk_decay = jnp.exp(a_cum[:, :, -1])

    def step(carry, inp):
        st, dec = inp
        return carry * dec[..., None, None] + st, carry

    init = jnp.zeros((bsz, G, H, P, D_STATE), jnp.float32)
    _, prev = lax.scan(step, init, (jnp.moveaxis(states, 1, 0), jnp.moveaxis(chunk_decay, 1, 0)))
    prev = jnp.moveaxis(prev, 0, 1)
    y_off = jnp.einsum('bclgn,bcghpn,bclgh->bclghp', Cc, prev, jnp.exp(a_cum))
    return (y_diag + y_off).reshape(bsz, L, G, H, P)


def token_mixers(h, w_in, conv_w, conv_b, dt_bias_f, dt_bias_b, a_log_f, a_log_b,
                 d_skip, gnorm_w, w_ssd_up, w_four, w_o):
    bsz, L, _ = h.shape
    proj = h @ w_in
    z, xbc, dt_f, dt_b, u_four, gate_logits = jnp.split(proj, SPLITS, axis=-1)
    xbc = lax.conv_general_dilated(
        xbc, conv_w.astype(xbc.dtype), window_strides=(1,), padding=[(CONV_PAD, CONV_PAD)],
        dimension_numbers=('NWC', 'WIO', 'NWC'), feature_group_count=CONV_DIM)
    xbc = jax.nn.silu(xbc + conv_b)
    xs, bm, cm = jnp.split(xbc, (D_INNER, D_INNER + SSD_GROUPS * D_STATE), axis=-1)
    xh = xs.astype(jnp.float32).reshape(bsz, L, SSD_GROUPS, HEADS_PER_GROUP, SSD_HEAD_DIM)
    bm = bm.astype(jnp.float32).reshape(bsz, L, SSD_GROUPS, D_STATE)
    cm = cm.astype(jnp.float32).reshape(bsz, L, SSD_GROUPS, D_STATE)
    dtf = jax.nn.softplus(dt_f.astype(jnp.float32) + dt_bias_f.astype(jnp.float32))
    dtb = jax.nn.softplus(dt_b.astype(jnp.float32) + dt_bias_b.astype(jnp.float32))
    dtf = dtf.reshape(bsz, L, SSD_GROUPS, HEADS_PER_GROUP)
    dtb = dtb.reshape(bsz, L, SSD_GROUPS, HEADS_PER_GROUP)
    a_f = -jnp.exp(a_log_f.astype(jnp.float32)).reshape(SSD_GROUPS, HEADS_PER_GROUP)
    a_b = -jnp.exp(a_log_b.astype(jnp.float32)).reshape(SSD_GROUPS, HEADS_PER_GROUP)
    flip = lambda t: jnp.flip(t, axis=1)
    y_fwd = ssd_chunked(xh, dtf, a_f, bm, cm)
    y_bwd = flip(ssd_chunked(flip(xh), flip(dtb), a_b, flip(bm), flip(cm)))
    d = d_skip.astype(jnp.float32).reshape(SSD_GROUPS, HEADS_PER_GROUP, 1)
    y = (y_fwd + y_bwd + xh * d).reshape(bsz, L, D_INNER)
    y_ssd = gated_group_rmsnorm(y, z, gnorm_w).astype(h.dtype) @ w_ssd_up
    u = u_four.astype(jnp.float32).reshape(bsz, L, FOUR_GROUPS, FOUR_GROUP_DIM)
    f = jnp.real(jnp.fft.fft2(u, axes=(1, 3), norm='ortho'))
    y_four = f.reshape(bsz, L, D_FOUR).astype(h.dtype) @ w_four
    g_ssd, g_four = jnp.split(jax.nn.sigmoid(gate_logits), 2, axis=-1)
    merged = g_ssd * y_ssd + g_four * y_four
    return merged @ w_o


def swiglu(t, w1, w3, w2):
    return (jax.nn.silu(t @ w1) * (t @ w3)) @ w2


def moe_swiglu(x, router, we1, we3, we2):
    bsz, L, d = x.shape
    t = x.reshape(-1, d)
    probs = jax.nn.softmax((t @ router).astype(jnp.float32), axis=-1)
    top_p, top_i = lax.top_k(probs, TOP_K)
    top_p = top_p / jnp.sum(top_p, axis=-1, keepdims=True)
    combine = jnp.einsum('tk,tke->te', top_p,
                         jax.nn.one_hot(top_i, N_EXPERTS, dtype=jnp.float32)).astype(x.dtype)
    out = jnp.zeros_like(t)
    for e in range(N_EXPERTS):
        out = out + combine[:, e:e + 1] * swiglu(t, we1[e], we3[e], we2[e])
    return out.reshape(bsz, L, d)


def setup_inputs(seed: int = 0) -> dict:
    key = jax.random.key(seed)
    keys = iter(jax.random.split(key, 128))
    nrm = lambda shape, scale: jax.random.normal(next(keys), shape, jnp.float32) * scale
    gain = lambda n: 1.0 + nrm((n,), 0.02)
    p = {}
    p['x_prompt'] = nrm((BATCH, SEQ, D_MODEL), 1.0)
    p['x_sample'] = nrm((DEC_BATCH, DEC_SEQ, D_MODEL), 1.0)
    p['ln_in_g'] = gain(D_MODEL)
    p['ln_in_b'] = nrm((D_MODEL,), 0.02)
    for l in range(DEPTH):
        p[f'w_in_{l}'] = nrm((D_MODEL, W_IN_COLS), D_MODEL ** -0.5)
        p[f'conv_w_{l}'] = nrm((CONV_K, 1, CONV_DIM), CONV_K ** -0.5)
        p[f'conv_b_{l}'] = nrm((CONV_DIM,), 0.01)
        for dname in ('f', 'b'):
            dt = jnp.exp(jax.random.uniform(next(keys), (SSD_HEADS,), jnp.float32,
                                            minval=math.log(1e-3), maxval=math.log(1e-1)))
            p[f'dt_bias_{dname}_{l}'] = dt + jnp.log(-jnp.expm1(-dt))
        for dname in ('f', 'b'):
            p[f'a_log_{dname}_{l}'] = jnp.log(jax.random.uniform(next(keys), (SSD_HEADS,), jnp.float32,
                                                                 minval=1.0, maxval=16.0))
        p[f'd_skip_{l}'] = 1.0 + nrm((SSD_HEADS,), 0.1)
        p[f'gnorm_w_{l}'] = gain(D_INNER)
        p[f'w_ssd_up_{l}'] = nrm((D_INNER, D_MODEL), D_INNER ** -0.5)
        p[f'w_four_{l}'] = nrm((D_FOUR, D_MODEL), D_FOUR ** -0.5)
        p[f'w_o_{l}'] = nrm((D_MODEL, D_MODEL), BETA * D_MODEL ** -0.5)
        p[f'ln1_g_{l}'] = gain(D_MODEL)
        p[f'ln1_b_{l}'] = nrm((D_MODEL,), 0.02)
        if l % 2 == 0:
            p[f'w1_{l}'] = nrm((D_MODEL, D_FF), D_MODEL ** -0.5)
            p[f'w3_{l}'] = nrm((D_MODEL, D_FF), D_MODEL ** -0.5)
            p[f'w2_{l}'] = nrm((D_FF, D_MODEL), BETA * D_FF ** -0.5)
        else:
            p[f'router_{l}'] = nrm((D_MODEL, N_EXPERTS), D_MODEL ** -0.5)
            p[f'we1_{l}'] = nrm((N_EXPERTS, D_MODEL, D_FF), D_MODEL ** -0.5)
            p[f'we3_{l}'] = nrm((N_EXPERTS, D_MODEL, D_FF), D_MODEL ** -0.5)
            p[f'we2_{l}'] = nrm((N_EXPERTS, D_FF, D_MODEL), BETA * D_FF ** -0.5)
        p[f'ln2_g_{l}'] = gain(D_MODEL)
        p[f'ln2_b_{l}'] = nrm((D_MODEL,), 0.02)
    return p


def reference(x_prompt, x_sample, ln_in_g, ln_in_b,
              w_in_0, conv_w_0, conv_b_0, dt_bias_f_0, dt_bias_b_0, a_log_f_0, a_log_b_0,
              d_skip_0, gnorm_w_0, w_ssd_up_0, w_four_0, w_o_0, ln1_g_0, ln1_b_0,
              w1_0, w3_0, w2_0, ln2_g_0, ln2_b_0,
              w_in_1, conv_w_1, conv_b_1, dt_bias_f_1, dt_bias_b_1, a_log_f_1, a_log_b_1,
              d_skip_1, gnorm_w_1, w_ssd_up_1, w_four_1, w_o_1, ln1_g_1, ln1_b_1,
              router_1, we1_1, we3_1, we2_1, ln2_g_1, ln2_b_1):
    mixer_params = [
        (w_in_0, conv_w_0, conv_b_0, dt_bias_f_0, dt_bias_b_0, a_log_f_0, a_log_b_0,
         d_skip_0, gnorm_w_0, w_ssd_up_0, w_four_0, w_o_0),
        (w_in_1, conv_w_1, conv_b_1, dt_bias_f_1, dt_bias_b_1, a_log_f_1, a_log_b_1,
         d_skip_1, gnorm_w_1, w_ssd_up_1, w_four_1, w_o_1),
    ]
    ln1 = [(ln1_g_0, ln1_b_0), (ln1_g_1, ln1_b_1)]
    ln2 = [(ln2_g_0, ln2_b_0), (ln2_g_1, ln2_b_1)]
    ffns = [
        lambda t: swiglu(t, w1_0, w3_0, w2_0),
        lambda t: moe_swiglu(t, router_1, we1_1, we3_1, we2_1),
    ]

    def trunk(x):
        x = layer_norm(x, ln_in_g, ln_in_b)
        for layer in range(DEPTH):
            x = layer_norm(ALPHA * x + token_mixers(x, *mixer_params[layer]), *ln1[layer])
            x = layer_norm(ALPHA * x + ffns[layer](x), *ln2[layer])
        return x

    y_prompt = trunk(x_prompt)
    y_sample = trunk(x_sample)
    return (y_prompt, y_sample)
```

```python
import functools
import math

import numpy as np
import jax
import jax.numpy as jnp
from jax import lax
from jax.experimental import pallas as pl
from jax.experimental.pallas import tpu as pltpu

F32 = jnp.float32
BF16 = jnp.bfloat16

SSD_GROUPS = 8
D_STATE = 128
CONV_K = 5
CHUNK = 128
FOUR_GROUPS = 8
FOUR_GROUP_DIM = 128
TOP_K = 2
DEPTH = 2
ALPHA = (2 * DEPTH) ** 0.25
LN_EPS = 1e-5
RMS_EPS = 1e-5

LANES = 128
BF16_SUBLANES = 16
VMEM_LIMIT_BYTES = 56 * 1024 * 1024

DT_ROWS = 16
BWD_ROW0 = 8
FFT_L2 = 128


def _cparams(*sem):
    return pltpu.CompilerParams(dimension_semantics=sem, vmem_limit_bytes=VMEM_LIMIT_BYTES)


def _tile(n, pref, align=8, divides=()):
    t = min(pref, n)
    t -= t % align
    while t > align:
        if n % t == 0 and all(d % t == 0 for d in divides):
            return t
        t -= align
    return align


def _sigmoid(x):
    return 1.0 / (1.0 + jnp.exp(-x))


def _silu(x):
    return x / (1.0 + jnp.exp(-x))


def _ln_rows(v, g, b):
    mu = jnp.mean(v, axis=-1, keepdims=True)
    d = v - mu
    var = jnp.mean(d * d, axis=-1, keepdims=True)
    return d * lax.rsqrt(var + LN_EPS) * g + b


def _split3(x):
    hi = x.astype(BF16)
    r1 = x - hi.astype(F32)
    mid = r1.astype(BF16)
    lo = (r1 - mid.astype(F32)).astype(BF16)
    return hi, mid, lo


def _ln_in_kernel(x_ref, g_ref, b_ref, xf_ref, xb_ref):
    y = _ln_rows(x_ref[...], g_ref[...], b_ref[...])
    xf_ref[...] = y
    xb_ref[...] = y.astype(BF16)


def _ln_in(x, g, b):
    T, D = x.shape
    tm = _tile(T, 512)
    row = pl.BlockSpec((tm, D), lambda i: (i, 0))
    vec = pl.BlockSpec((1, D), lambda i: (0, 0))
    return pl.pallas_call(
        _ln_in_kernel,
        grid=(T // tm,),
        in_specs=[row, vec, vec],
        out_specs=[row, row],
        out_shape=[jax.ShapeDtypeStruct((T, D), F32), jax.ShapeDtypeStruct((T, D), BF16)],
        compiler_params=_cparams("parallel"),
        name="ln_in",
    )(x, g.reshape(1, D), b.reshape(1, D))


def _res_ln_kernel(x_ref, y_ref, g_ref, b_ref, xf_ref, xb_ref):
    v = ALPHA * x_ref[...] + y_ref[...].astype(F32)
    y = _ln_rows(v, g_ref[...], b_ref[...])
    xf_ref[...] = y
    xb_ref[...] = y.astype(BF16)


def _res_ln(x, y, g, b):
    T, D = x.shape
    tm = _tile(T, 512)
    row = pl.BlockSpec((tm, D), lambda i: (i, 0))
    vec = pl.BlockSpec((1, D), lambda i: (0, 0))
    return pl.pallas_call(
        _res_ln_kernel,
        grid=(T // tm,),
        in_specs=[row, row, vec, vec],
        out_specs=[row, row],
        out_shape=[jax.ShapeDtypeStruct((T, D), F32), jax.ShapeDtypeStruct((T, D), BF16)],
        compiler_params=_cparams("parallel"),
        name="res_ln",
    )(x, y, g.reshape(1, D), b.reshape(1, D))


def _combine_ln_kernel(x_ref, ya_ref, yb_ref, w_ref, g_ref, b_ref, xf_ref, xb_ref):
    w = w_ref[...]
    mix = w[:, 0:1] * ya_ref[...].astype(F32) + w[:, 1:2] * yb_ref[...].astype(F32)
    y = _ln_rows(ALPHA * x_ref[...] + mix, g_ref[...], b_ref[...])
    xf_ref[...] = y
    xb_ref[...] = y.astype(BF16)


def _combine_ln(x, ya, yb, w, g, b):
    T, D = x.shape
    tm = _tile(T, 512)
    row = pl.BlockSpec((tm, D), lambda i: (i, 0))
    wsp = pl.BlockSpec((tm, LANES), lambda i: (i, 0))
    vec = pl.BlockSpec((1, D), lambda i: (0, 0))
    return pl.pallas_call(
        _combine_ln_kernel,
        grid=(T // tm,),
        in_specs=[row, row, row, wsp, vec, vec],
        out_specs=[row, row],
        out_shape=[jax.ShapeDtypeStruct((T, D), F32), jax.ShapeDtypeStruct((T, D), BF16)],
        compiler_params=_cparams("parallel"),
        name="combine_ln",
    )(x, ya, yb, w, g.reshape(1, D), b.reshape(1, D))


def _mm_kernel(a_ref, w_ref, o_ref):
    o_ref[...] = jnp.dot(a_ref[...], w_ref[...], preferred_element_type=F32).astype(o_ref.dtype)


def _mm(a, w, out_dtype, tm_pref=1024, tn_pref=1024):
    M, K = a.shape
    N = w.shape[1]
    tm = _tile(M, tm_pref)
    tn = _tile(N, tn_pref, LANES)
    return pl.pallas_call(
        _mm_kernel,
        grid=(M // tm, N // tn),
        in_specs=[pl.BlockSpec((tm, K), lambda i, j: (i, 0)), pl.BlockSpec((K, tn), lambda i, j: (0, j))],
        out_specs=pl.BlockSpec((tm, tn), lambda i, j: (i, j)),
        out_shape=jax.ShapeDtypeStruct((M, N), out_dtype),
        compiler_params=_cparams("parallel", "arbitrary"),
        name="mm",
    )(a, w)


def _gate_mm_kernel(a_ref, w_ref, g_ref, o_ref):
    y = jnp.dot(a_ref[...], w_ref[...], preferred_element_type=F32)
    o_ref[...] = (_sigmoid(g_ref[...].astype(F32)) * y).astype(o_ref.dtype)


def _gate_mm(a, w, proj, gate_off):
    M, K = a.shape
    N = w.shape[1]
    tm = _tile(M, 1024)
    tn = _tile(N, 1024, LANES, divides=(gate_off,))
    goff = gate_off // tn
    return pl.pallas_call(
        _gate_mm_kernel,
        grid=(M // tm, N // tn),
        in_specs=[pl.BlockSpec((tm, K), lambda i, j: (i, 0)),
                  pl.BlockSpec((K, tn), lambda i, j: (0, j)),
                  pl.BlockSpec((tm, tn), lambda i, j: (i, goff + j))],
        out_specs=pl.BlockSpec((tm, tn), lambda i, j: (i, j)),
        out_shape=jax.ShapeDtypeStruct((M, N), BF16),
        compiler_params=_cparams("parallel", "arbitrary"),
        name="gate_mm",
    )(a, w, proj)


def _mm_res_ln_kernel(a_ref, w_ref, x_ref, g_ref, b_ref, xf_ref, xb_ref):
    y = jnp.dot(a_ref[...], w_ref[...], preferred_element_type=F32)
    out = _ln_rows(ALPHA * x_ref[...] + y, g_ref[...], b_ref[...])
    xf_ref[...] = out
    xb_ref[...] = out.astype(BF16)


def _mm_res_ln(a, w, x, g, b):
    M, K = a.shape
    D = w.shape[1]
    tm = _tile(M, 512)
    row = pl.BlockSpec((tm, D), lambda i: (i, 0))
    vec = pl.BlockSpec((1, D), lambda i: (0, 0))
    return pl.pallas_call(
        _mm_res_ln_kernel,
        grid=(M // tm,),
        in_specs=[pl.BlockSpec((tm, K), lambda i: (i, 0)), pl.BlockSpec((K, D), lambda i: (0, 0)), row, vec, vec],
        out_specs=[row, row],
        out_shape=[jax.ShapeDtypeStruct((M, D), F32), jax.ShapeDtypeStruct((M, D), BF16)],
        compiler_params=_cparams("parallel"),
        name="mm_res_ln",
    )(a, w, x, g.reshape(1, D), b.reshape(1, D))


def _conv_silu_phase(src_ref, w_ref, bias_ref, dst_ref, L):
    nc = L // CHUNK
    halo = BF16_SUBLANES
    n_ext = CHUNK + 2 * halo
    W = src_ref.shape[1]
    w = w_ref[...]
    bias = bias_ref[...]

    def body(c, carry):
        s0 = pl.multiple_of(c * CHUNK, CHUNK)
        sp = pl.multiple_of(jnp.maximum(s0 - halo, 0), halo)
        sn = pl.multiple_of(jnp.minimum(s0 + CHUNK, L - halo), halo)
        cur = src_ref[pl.ds(s0, CHUNK), :].astype(F32)
        prev = jnp.where(c > 0, src_ref[pl.ds(sp, halo), :].astype(F32), 0.0)
        nxt = jnp.where(c < nc - 1, src_ref[pl.ds(sn, halo), :].astype(F32), 0.0)
        ext = jnp.concatenate([prev, cur, nxt], axis=0)
        acc = jnp.broadcast_to(bias, (CHUNK, W))
        for k in range(CONV_K):
            shift = (n_ext - halo - (k - CONV_K // 2)) % n_ext
            acc = acc + w[k:k + 1, :] * pltpu.roll(ext, shift, 0)[:CHUNK]
        dst_ref[pl.ds(s0, CHUNK), :] = _silu(acc).astype(dst_ref.dtype)
        return carry

    lax.fori_loop(0, nc, body, 0)


def _ssd_kernel(z_ref, xs_ref, b_ref, c_ref, dt_ref, cwx_ref, cwb_ref, cwc_ref, cbx_ref, cbb_ref, cbc_ref,
                dtb_ref, alc_ref, all_ref, dsk_ref, gnw_ref, o_ref, xs_s, b_s, c_s, st_s, *, L, P):
    nc = L // CHUNK
    gw = xs_s.shape[1]
    hpg = gw // P

    _conv_silu_phase(xs_ref, cwx_ref, cbx_ref, xs_s, L)
    _conv_silu_phase(b_ref, cwb_ref, cbb_ref, b_s, L)
    _conv_silu_phase(c_ref, cwc_ref, cbc_ref, c_s, L)

    ri = lax.broadcasted_iota(jnp.int32, (CHUNK, CHUNK), 0)
    ci = lax.broadcasted_iota(jnp.int32, (CHUNK, CHUNK), 1)
    incl_upper = (ri <= ci).astype(BF16)
    ej = lax.broadcasted_iota(jnp.int32, (CHUNK, gw), 0)
    ec = lax.broadcasted_iota(jnp.int32, (CHUNK, gw), 1)
    zpad = jnp.zeros((CHUNK - DT_ROWS, CHUNK), F32)
    a_c = -jnp.exp(alc_ref[...])
    a_l = -jnp.exp(all_ref[...])
    dt_bias = dtb_ref[...]

    def ssd_pass(bwd):
        row0 = BWD_ROW0 if bwd else 0
        expand = ((ec >= (ej - row0) * P) & (ec < (ej - row0 + 1) * P)).astype(BF16)
        mask = (ci >= ri) if bwd else (ci <= ri)
        st_s[...] = jnp.zeros_like(st_s)

        def body(i, carry):
            c = (nc - 1 - i) if bwd else i
            s0 = pl.multiple_of(c * CHUNK, CHUNK)
            raw = dt_ref[:, pl.ds(s0, CHUNK)] + dt_bias
            dtv = jnp.maximum(raw, 0.0) + jnp.log1p(jnp.exp(-jnp.abs(raw)))
            a_row = dtv * a_c
            hi, mid, lo = _split3(a_row)
            acum_row = (jnp.dot(hi, incl_upper, preferred_element_type=F32)
                        + jnp.dot(mid, incl_upper, preferred_element_type=F32)
                        + jnp.dot(lo, incl_upper, preferred_element_type=F32))
            acum_col = jnp.concatenate([acum_row, zpad], axis=0).T
            dt_col = jnp.concatenate([dtv, zpad], axis=0).T
            total = acum_col[CHUNK - 1:CHUNK, :]
            if bwd:
                e_row = acum_row - a_row
                e_col = acum_col - dt_col * a_l
                f_state = dt_col * jnp.exp(e_col)
                f_off = jnp.exp(total - e_col)
            else:
                e_row = acum_row
                e_col = acum_col
                f_state = dt_col * jnp.exp(total - e_col)
                f_off = jnp.exp(e_col)
            lhs = jnp.concatenate([dt_col, f_state, f_off], axis=0).astype(BF16)
            ex = jnp.dot(lhs, expand, preferred_element_type=F32)
            dt_x = ex[:CHUNK]
            fs_x = ex[CHUNK:2 * CHUNK]
            fo_x = ex[2 * CHUNK:]
            xc = xs_s[pl.ds(s0, CHUNK), :].astype(F32)
            x_dt = (xc * dt_x).astype(BF16)
            x_st = (xc * fs_x).astype(BF16)
            bc = b_s[pl.ds(s0, CHUNK), :]
            cc = c_s[pl.ds(s0, CHUNK), :]
            cb = lax.dot_general(cc, bc, (((1,), (1,)), ((), ())), preferred_element_type=F32)
            cbm = jnp.where(mask, cb, 0.0)
            state = st_s[...]
            y = jnp.dot(cc, state.astype(BF16), preferred_element_type=F32) * fo_x
            parts = []
            for h in range(hpg):
                j = row0 + h
                col = e_col[:, j:j + 1]
                row = e_row[j:j + 1, :]
                seg = (row - col) if bwd else (col - row)
                lm = (jnp.exp(jnp.minimum(seg, 0.0)) * cbm).astype(BF16)
                parts.append(jnp.dot(lm, x_dt[:, h * P:(h + 1) * P], preferred_element_type=F32))
            y = y + jnp.concatenate(parts, axis=1)
            e_total = fo_x[0:1, :] if bwd else fo_x[CHUNK - 1:CHUNK, :]
            bt = bc.astype(F32).T.astype(BF16)
            st_s[...] = state * e_total + jnp.dot(bt, x_st, preferred_element_type=F32)
            if bwd:
                y = y + o_ref[pl.ds(s0, CHUNK), :].astype(F32) + xc * dsk_ref[...]
                v = y * _silu(z_ref[pl.ds(s0, CHUNK), :].astype(F32))
                ms = jnp.mean(v * v, axis=-1, keepdims=True)
                o_ref[pl.ds(s0, CHUNK), :] = (v * lax.rsqrt(ms + RMS_EPS) * gnw_ref[...]).astype(o_ref.dtype)
            else:
                o_ref[pl.ds(s0, CHUNK), :] = y.astype(o_ref.dtype)
            return carry

        lax.fori_loop(0, nc, body, 0)

    ssd_pass(False)
    ssd_pass(True)


def _ssd(proj3, dt4, conv_w, conv_b, dt_bias_c, a_log_c, a_log_l, d_skip_x, gnorm_w, DI, P):
    B, L, _ = proj3.shape
    G = SSD_GROUPS
    gw = DI // G
    sblk = D_STATE // LANES
    assert D_STATE == LANES and L % CHUNK == 0 and gw % LANES == 0 and gw // P <= BWD_ROW0
    xoff = DI // gw
    boff = 2 * DI // LANES
    coff = boff + G * sblk
    cwb_off = DI // LANES
    block_bytes = L * (3 * gw + 2 * D_STATE) * 2
    single = 3 * block_bytes > VMEM_LIMIT_BYTES

    def big(shape, imap):
        if single:
            return pl.BlockSpec(shape, imap, pipeline_mode=pl.Buffered(1))
        return pl.BlockSpec(shape, imap)

    in_specs = [
        big((None, L, gw), lambda b, g: (b, 0, g)),
        big((None, L, gw), lambda b, g: (b, 0, xoff + g)),
        big((None, L, D_STATE), lambda b, g: (b, 0, boff + g)),
        big((None, L, D_STATE), lambda b, g: (b, 0, coff + g)),
        pl.BlockSpec((None, None, DT_ROWS, L), lambda b, g: (b, g, 0, 0)),
        pl.BlockSpec((CONV_K, gw), lambda b, g: (0, g)),
        pl.BlockSpec((CONV_K, D_STATE), lambda b, g: (0, cwb_off + g)),
        pl.BlockSpec((CONV_K, D_STATE), lambda b, g: (0, cwb_off + G + g)),
        pl.BlockSpec((1, gw), lambda b, g: (0, g)),
        pl.BlockSpec((1, D_STATE), lambda b, g: (0, cwb_off + g)),
        pl.BlockSpec((1, D_STATE), lambda b, g: (0, cwb_off + G + g)),
        pl.BlockSpec((None, DT_ROWS, 1), lambda b, g: (g, 0, 0)),
        pl.BlockSpec((None, DT_ROWS, 1), lambda b, g: (g, 0, 0)),
        pl.BlockSpec((None, 1, LANES), lambda b, g: (g, 0, 0)),
        pl.BlockSpec((1, gw), lambda b, g: (0, g)),
        pl.BlockSpec((1, gw), lambda b, g: (0, g)),
    ]
    return pl.pallas_call(
        functools.partial(_ssd_kernel, L=L, P=P),
        grid=(B, G),
        in_specs=in_specs,
        out_specs=pl.BlockSpec((None, L, gw), lambda b, g: (b, 0, g)),
        out_shape=jax.ShapeDtypeStruct((B, L, DI), BF16),
        scratch_shapes=[pltpu.VMEM((L, gw), BF16), pltpu.VMEM((L, D_STATE), BF16), pltpu.VMEM((L, D_STATE), BF16),
                        pltpu.VMEM((D_STATE, gw), F32)],
        compiler_params=_cparams("parallel", "parallel"),
        name="ssd",
    )(proj3, proj3, proj3, proj3, dt4, conv_w, conv_w, conv_w, conv_b, conv_b, conv_b,
      dt_bias_c, a_log_c, a_log_l, d_skip_x, gnorm_w)


def _fft1_kernel(w_ref, tw_ref, u_ref, o_ref, *, L1, nl2, C):
    res = jnp.dot(w_ref[...], u_ref[...], preferred_element_type=F32)
    rep = C // LANES
    for q in range(nl2):
        a = res[:L1, q * C:(q + 1) * C]
        b = res[L1:, q * C:(q + 1) * C]
        twr = jnp.concatenate([tw_ref[0, q]] * rep, axis=1)
        twi = jnp.concatenate([tw_ref[1, q]] * rep, axis=1)
        o_ref[0, q] = (a * twr - b * twi).astype(o_ref.dtype)
        o_ref[1, q] = (a * twi + b * twr).astype(o_ref.dtype)


def _fft_stage1(u3, L1):
    B, L, C = u3.shape
    L2 = L // L1
    nl2 = 8
    k1 = np.arange(L1)
    ang1 = 2.0 * np.pi * np.outer(k1, np.arange(L1)) / L1
    w1 = jnp.asarray(np.concatenate([np.cos(ang1), -np.sin(ang1)], axis=0), F32).astype(BF16)
    angt = 2.0 * np.pi * np.outer(np.arange(L2), k1) / L
    tw = np.stack([np.cos(angt), -np.sin(angt)], axis=0)[..., None]
    tw = jnp.asarray(np.broadcast_to(tw, (2, L2, L1, LANES)), F32)
    uv = u3.reshape(B, L1, L2 * C)
    return pl.pallas_call(
        functools.partial(_fft1_kernel, L1=L1, nl2=nl2, C=C),
        grid=(B, L2 // nl2),
        in_specs=[pl.BlockSpec((2 * L1, L1), lambda b, j: (0, 0)),
                  pl.BlockSpec((2, nl2, L1, LANES), lambda b, j: (0, j, 0, 0)),
                  pl.BlockSpec((None, L1, nl2 * C), lambda b, j: (b, 0, j))],
        out_specs=pl.BlockSpec((None, 2, nl2, L1, C), lambda b, j: (b, 0, j, 0, 0)),
        out_shape=jax.ShapeDtypeStruct((B, 2, L2, L1, C), BF16),
        compiler_params=_cparams("parallel", "parallel"),
        name="fft_stage1",
    )(w1, tw, uv)


def _fft_stage2(g5):
    B, _, L2, L1, C = g5.shape
    ang = 2.0 * np.pi * np.outer(np.arange(L2), np.arange(L2)) / L2
    cs, sn = np.cos(ang), np.sin(ang)
    w2 = jnp.asarray(np.block([[cs, sn], [-sn, cs]]), F32).astype(BF16)
    N = L1 * C
    tn = _tile(N, 4096, LANES)
    gv = g5.reshape(B, 2 * L2, N)
    y = pl.pallas_call(
        _mm_kernel,
        grid=(B, N // tn),
        in_specs=[pl.BlockSpec((2 * L2, 2 * L2), lambda b, j: (0, 0)),
                  pl.BlockSpec((None, 2 * L2, tn), lambda b, j: (b, 0, j))],
        out_specs=pl.BlockSpec((None, 2 * L2, tn), lambda b, j: (b, 0, j)),
        out_shape=jax.ShapeDtypeStruct((B, 2 * L2, N), BF16),
        compiler_params=_cparams("parallel", "parallel"),
        name="fft_stage2",
    )(w2, gv)
    return y.reshape(B, 2, L2 * L1, C)


def _four_out_kernel(y_ref, cs_ref, wf_ref, g_ref, t_ref, o_ref, f_s, *, scale):
    @pl.when(pl.program_id(2) == 0)
    def _():
        yr = y_ref[0]
        yi = y_ref[1]
        cs = cs_ref[...]
        parts = []
        for g in range(FOUR_GROUPS):
            sl = slice(g * FOUR_GROUP_DIM, (g + 1) * FOUR_GROUP_DIM)
            lhs = jnp.concatenate([yr[:, sl], yi[:, sl]], axis=1)
            parts.append(jnp.dot(lhs, cs, preferred_element_type=F32))
        f_s[...] = (jnp.concatenate(parts, axis=1) * scale).astype(f_s.dtype)

    yf = jnp.dot(f_s[...], wf_ref[...], preferred_element_type=F32)
    o_ref[...] = (t_ref[...].astype(F32) + _sigmoid(g_ref[...].astype(F32)) * yf).astype(o_ref.dtype)


def _four_out(y4, w_four, proj, gate_off, t1):
    B, _, L, C = y4.shape
    D = w_four.shape[1]
    tm = _tile(L, 512)
    tn = _tile(D, 1024, LANES, divides=(gate_off,))
    goff = gate_off // tn
    nti = L // tm
    ang = 2.0 * np.pi * np.outer(np.arange(FOUR_GROUP_DIM), np.arange(FOUR_GROUP_DIM)) / FOUR_GROUP_DIM
    cs = jnp.asarray(np.concatenate([np.cos(ang), np.sin(ang)], axis=0), F32).astype(BF16)
    scale = 1.0 / math.sqrt(L * FOUR_GROUP_DIM)
    return pl.pallas_call(
        functools.partial(_four_out_kernel, scale=scale),
        grid=(B, nti, D // tn),
        in_specs=[pl.BlockSpec((None, 2, tm, C), lambda b, i, j: (b, 0, i, 0)),
                  pl.BlockSpec((2 * FOUR_GROUP_DIM, FOUR_GROUP_DIM), lambda b, i, j: (0, 0)),
                  pl.BlockSpec((C, tn), lambda b, i, j: (0, j)),
                  pl.BlockSpec((tm, tn), lambda b, i, j: (b * nti + i, goff + j)),
                  pl.BlockSpec((tm, tn), lambda b, i, j: (b * nti + i, j))],
        out_specs=pl.BlockSpec((tm, tn), lambda b, i, j: (b * nti + i, j)),
        out_shape=jax.ShapeDtypeStruct((B * L, D), BF16),
        scratch_shapes=[pltpu.VMEM((tm, C), BF16)],
        compiler_params=_cparams("parallel", "parallel", "arbitrary"),
        name="four_out",
    )(y4, cs, w_four, proj, t1)


def _ffn_kernel(te_ref, act_ref, x_ref, w1_ref, w3_ref, w2_ref, o_ref, acc_s):
    i = pl.program_id(0)
    j = pl.program_id(1)
    nj = pl.num_programs(1)

    @pl.when(j == 0)
    def _():
        acc_s[...] = jnp.zeros_like(acc_s)

    @pl.when(act_ref[i] > 0)
    def _():
        x = x_ref[...]
        h1 = jnp.dot(x, w1_ref[...], preferred_element_type=F32)
        h3 = jnp.dot(x, w3_ref[...], preferred_element_type=F32)
        h = (_silu(h1) * h3).astype(BF16)
        acc_s[...] += jnp.dot(h, w2_ref[...], preferred_element_type=F32)

    @pl.when(j == nj - 1)
    def _():
        o_ref[...] = acc_s[...].astype(o_ref.dtype)


def _ffn(x, w1, w3, w2, tile_expert, tile_active, tm):
    R, D = x.shape
    E, _, F = w1.shape
    tf = _tile(F, 512, LANES)
    nf = F // tf

    def wcol(i, j, te, act):
        return (te[i], 0, jnp.where(act[i] > 0, j, nf - 1))

    def wrow(i, j, te, act):
        return (te[i], jnp.where(act[i] > 0, j, nf - 1), 0)

    grid_spec = pltpu.PrefetchScalarGridSpec(
        num_scalar_prefetch=2,
        grid=(R // tm, nf),
        in_specs=[pl.BlockSpec((tm, D), lambda i, j, te, act: (i, 0)),
                  pl.BlockSpec((None, D, tf), wcol),
                  pl.BlockSpec((None, D, tf), wcol),
                  pl.BlockSpec((None, tf, D), wrow)],
        out_specs=pl.BlockSpec((tm, D), lambda i, j, te, act: (i, 0)),
        scratch_shapes=[pltpu.VMEM((tm, D), F32)],
    )
    return pl.pallas_call(
        _ffn_kernel,
        grid_spec=grid_spec,
        out_shape=jax.ShapeDtypeStruct((R, D), BF16),
        compiler_params=_cparams("parallel", "arbitrary"),
        name="ffn",
    )(tile_expert, tile_active, x, w1, w3, w2)


def _router_kernel(x_ref, r_ref, w_ref, i_ref, *, E):
    xh, xm, _ = _split3(x_ref[...])
    rh, rm, _ = _split3(r_ref[...])
    logits = (jnp.dot(xh, rh, preferred_element_type=F32) + jnp.dot(xm, rh, preferred_element_type=F32)
              + jnp.dot(xh, rm, preferred_element_type=F32))
    lane = lax.broadcasted_iota(jnp.int32, logits.shape, 1).astype(F32)
    logits = jnp.where(lane < E, logits, -jnp.inf)
    ex = jnp.exp(logits - jnp.max(logits, axis=-1, keepdims=True))
    probs = ex / jnp.sum(ex, axis=-1, keepdims=True)
    m1 = jnp.max(probs, axis=-1, keepdims=True)
    i1 = jnp.min(jnp.where(probs == m1, lane, float(LANES)), axis=-1, keepdims=True)
    rest = jnp.where(lane == i1, -1.0, probs)
    m2 = jnp.max(rest, axis=-1, keepdims=True)
    i2 = jnp.min(jnp.where(rest == m2, lane, float(LANES)), axis=-1, keepdims=True)
    den = m1 + m2
    w_ref[...] = jnp.where(lane == 0.0, m1 / den, jnp.where(lane == 1.0, m2 / den, 0.0))
    i_ref[...] = jnp.where(lane == 0.0, i1, jnp.where(lane == 1.0, i2, 0.0)).astype(jnp.int32)


def _router(x, router):
    T, D = x.shape
    E = router.shape[1]
    tm = _tile(T, 512)
    rp = jnp.zeros((D, LANES), F32).at[:, :E].set(router)
    row = pl.BlockSpec((tm, LANES), lambda i: (i, 0))
    return pl.pallas_call(
        functools.partial(_router_kernel, E=E),
        grid=(T // tm,),
        in_specs=[pl.BlockSpec((tm, D), lambda i: (i, 0)), pl.BlockSpec((D, LANES), lambda i: (0, 0))],
        out_specs=[row, row],
        out_shape=[jax.ShapeDtypeStruct((T, LANES), F32), jax.ShapeDtypeStruct((T, LANES), jnp.int32)],
        compiler_params=_cparams("parallel"),
        name="router",
    )(x, rp)


def _moe(xf, xb, router, we1, we3, we2, ln_g, ln_b):
    T, D = xf.shape
    E = we1.shape[0]
    tm = _tile(T, 1024)
    wt, idx = _router(xf, router)
    flat_e = idx[:, :TOP_K].reshape(-1)
    onehot = (flat_e[:, None] == jnp.arange(E, dtype=jnp.int32)[None, :]).astype(jnp.int32)
    rank = jnp.sum((jnp.cumsum(onehot, axis=0) - onehot) * onehot, axis=1)
    counts = jnp.sum(onehot, axis=0)
    padded = ((counts + tm - 1) // tm) * tm
    ends = jnp.cumsum(padded)
    pos = (ends - padded)[flat_e] + rank
    n_tiles = (TOP_K * T) // tm + E
    src = jnp.zeros((n_tiles * tm,), jnp.int32).at[pos].set(jnp.arange(TOP_K * T, dtype=jnp.int32) // TOP_K)
    starts = jnp.arange(n_tiles, dtype=jnp.int32) * tm
    tile_expert = jnp.minimum(jnp.searchsorted(ends, starts, side="right"), E - 1).astype(jnp.int32)
    tile_active = (starts < ends[-1]).astype(jnp.int32)
    x_sorted = jnp.take(xb, src, axis=0)
    y_sorted = _ffn(x_sorted, we1, we3, we2, tile_expert, tile_active, tm)
    pos2 = pos.reshape(T, TOP_K)
    ya = jnp.take(y_sorted, pos2[:, 0], axis=0)
    yb = jnp.take(y_sorted, pos2[:, 1], axis=0)
    return _combine_ln(xf, ya, yb, wt, ln_g, ln_b)


def _prep_mixer(w_in, conv_w, conv_b, dt_bias_f, dt_bias_b, a_log_f, a_log_b, d_skip, gnorm_w,
                w_ssd_up, w_four, w_o):
    D = w_in.shape[0]
    DI = w_ssd_up.shape[0]
    CD = conv_b.shape[0]
    H = dt_bias_f.shape[0]
    DF = w_four.shape[0]
    G = SSD_GROUPS
    hpg = H // G
    o_dtf = DI + CD
    o_dtb = o_dtf + H
    o_u = o_dtb + H
    o_g = o_u + DF
    w_main = jnp.concatenate([w_in[:, :o_dtf], w_in[:, o_g:]], axis=1).astype(BF16)
    w_u = w_in[:, o_u:o_g].astype(BF16)

    def dt_rows(f, b, fill=0.0):
        lead = f.shape[:-1]
        out = jnp.full(lead + (G, DT_ROWS), fill, F32)
        out = out.at[..., :hpg].set(f.reshape(lead + (G, hpg)))
        return out.at[..., BWD_ROW0:BWD_ROW0 + hpg].set(b.reshape(lead + (G, hpg)))

    w_dt = dt_rows(w_in[:, o_dtf:o_dtb], w_in[:, o_dtb:o_u]).reshape(D, G * DT_ROWS).astype(BF16)
    dt_bias = dt_rows(dt_bias_f, dt_bias_b)
    a_log = dt_rows(a_log_f, a_log_b)
    a_log_l = jnp.zeros((G, 1, LANES), F32).at[:, 0, :DT_ROWS].set(a_log)
    return dict(
        w_main=w_main, w_u=w_u, w_dt=w_dt,
        conv_w=conv_w.reshape(CONV_K, CD), conv_b=conv_b.reshape(1, CD),
        dt_bias_c=dt_bias[..., None], a_log_c=a_log[..., None], a_log_l=a_log_l,
        d_skip_x=jnp.repeat(d_skip, DI // H).reshape(1, DI), gnorm_w=gnorm_w.reshape(1, DI),
        w_ssd_up=w_ssd_up.astype(BF16), w_four=w_four.astype(BF16), w_o=w_o.astype(BF16),
        DI=DI, P=DI // H, gate_off=o_dtf, D=D)


def _token_mixer(xf, xb, B, L, mp, ln_g, ln_b):
    T = B * L
    G = SSD_GROUPS
    DI, D = mp["DI"], mp["D"]
    proj = _mm(xb, mp["w_main"], BF16)
    u = _mm(xb, mp["w_u"], BF16)
    dt = _mm(xb, mp["w_dt"], F32)
    dt4 = dt.reshape(B, L, G, DT_ROWS).transpose(0, 2, 3, 1)
    v = _ssd(proj.reshape(B, L, -1), dt4, mp["conv_w"], mp["conv_b"], mp["dt_bias_c"], mp["a_log_c"],
             mp["a_log_l"], mp["d_skip_x"], mp["gnorm_w"], DI, mp["P"])
    t1 = _gate_mm(v.reshape(T, DI), mp["w_ssd_up"], proj, mp["gate_off"])
    g5 = _fft_stage1(u.reshape(B, L, -1), L // FFT_L2)
    y4 = _fft_stage2(g5)
    merged = _four_out(y4, mp["w_four"], proj, mp["gate_off"] + D, t1)
    return _mm_res_ln(merged, mp["w_o"], xf, ln_g, ln_b)


def kernel(x_prompt, x_sample, ln_in_g, ln_in_b, w_in_0, conv_w_0, conv_b_0, dt_bias_f_0, dt_bias_b_0, a_log_f_0, a_log_b_0, d_skip_0, gnorm_w_0, w_ssd_up_0, w_four_0, w_o_0, ln1_g_0, ln1_b_0, w1_0, w3_0, w2_0, ln2_g_0, ln2_b_0, w_in_1, conv_w_1, conv_b_1, dt_bias_f_1, dt_bias_b_1, a_log_f_1, a_log_b_1, d_skip_1, gnorm_w_1, w_ssd_up_1, w_four_1, w_o_1, ln1_g_1, ln1_b_1, router_1, we1_1, we3_1, we2_1, ln2_g_1, ln2_b_1):
    mp0 = _prep_mixer(w_in_0, conv_w_0, conv_b_0, dt_bias_f_0, dt_bias_b_0, a_log_f_0, a_log_b_0,
                      d_skip_0, gnorm_w_0, w_ssd_up_0, w_four_0, w_o_0)
    mp1 = _prep_mixer(w_in_1, conv_w_1, conv_b_1, dt_bias_f_1, dt_bias_b_1, a_log_f_1, a_log_b_1,
                      d_skip_1, gnorm_w_1, w_ssd_up_1, w_four_1, w_o_1)
    w1 = w1_0.astype(BF16)[None]
    w3 = w3_0.astype(BF16)[None]
    w2 = w2_0.astype(BF16)[None]
    we1 = we1_1.astype(BF16)
    we3 = we3_1.astype(BF16)
    we2 = we2_1.astype(BF16)

    def trunk(x):
        B, L, D = x.shape
        T = B * L
        xf, xb = _ln_in(x.reshape(T, D), ln_in_g, ln_in_b)
        xf, xb = _token_mixer(xf, xb, B, L, mp0, ln1_g_0, ln1_b_0)
        tm = _tile(T, 1024)
        n_tiles = T // tm
        ffn = _ffn(xb, w1, w3, w2, jnp.zeros((n_tiles,), jnp.int32), jnp.ones((n_tiles,), jnp.int32), tm)
        xf, xb = _res_ln(xf, ffn, ln2_g_0, ln2_b_0)
        xf, xb = _token_mixer(xf, xb, B, L, mp1, ln1_g_1, ln1_b_1)
        xf, _ = _moe(xf, xb, router_1, we1, we3, we2, ln2_g_1, ln2_b_1)
        return xf.reshape(B, L, D)

    return (trunk(x_prompt), trunk(x_sample))
```

```python
import functools
import math

import numpy as np
import jax
import jax.numpy as jnp
from jax import lax
from jax.experimental import pallas as pl
from jax.experimental.pallas import tpu as pltpu

F32 = jnp.float32
BF16 = jnp.bfloat16

SSD_GROUPS = 8
D_STATE = 128
CONV_K = 5
CHUNK = 128
FOUR_GROUPS = 8
FOUR_GROUP_DIM = 128
TOP_K = 2
DEPTH = 2
ALPHA = (2 * DEPTH) ** 0.25
LN_EPS = 1e-5
RMS_EPS = 1e-5

LANES = 128
BF16_SUBLANES = 16
VMEM_LIMIT_BYTES = 56 * 1024 * 1024

DT_ROWS = 16
BWD_ROW0 = 8
FFT_L2 = 128


def _cparams(*sem):
    return pltpu.CompilerParams(dimension_semantics=sem, vmem_limit_bytes=VMEM_LIMIT_BYTES)


def _tile(n, pref, align=8, divides=()):
    t = min(pref, n)
    t -= t % align
    while t > align:
        if n % t == 0 and all(d % t == 0 for d in divides):
            return t
        t -= align
    return align


def _sigmoid(x):
    return 1.0 / (1.0 + jnp.exp(-x))


def _silu(x):
    return x / (1.0 + jnp.exp(-x))


def _ln_rows(v, g, b):
    mu = jnp.mean(v, axis=-1, keepdims=True)
    d = v - mu
    var = jnp.mean(d * d, axis=-1, keepdims=True)
    return d * lax.rsqrt(var + LN_EPS) * g + b


def _split3(x):
    hi = x.astype(BF16)
    r1 = x - hi.astype(F32)
    mid = r1.astype(BF16)
    lo = (r1 - mid.astype(F32)).astype(BF16)
    return hi, mid, lo


def _ln_in_kernel(x_ref, g_ref, b_ref, xf_ref, xb_ref):
    y = _ln_rows(x_ref[...], g_ref[...], b_ref[...])
    xf_ref[...] = y
    xb_ref[...] = y.astype(BF16)


def _ln_in(x, g, b):
    T, D = x.shape
    tm = _tile(T, 512)
    row = pl.BlockSpec((tm, D), lambda i: (i, 0))
    vec = pl.BlockSpec((1, D), lambda i: (0, 0))
    return pl.pallas_call(
        _ln_in_kernel,
        grid=(T // tm,),
        in_specs=[row, vec, vec],
        out_specs=[row, row],
        out_shape=[jax.ShapeDtypeStruct((T, D), F32), jax.ShapeDtypeStruct((T, D), BF16)],
        compiler_params=_cparams("parallel"),
        name="ln_in",
    )(x, g.reshape(1, D), b.reshape(1, D))


def _res_ln_kernel(x_ref, y_ref, g_ref, b_ref, xf_ref, xb_ref):
    v = ALPHA * x_ref[...] + y_ref[...].astype(F32)
    y = _ln_rows(v, g_ref[...], b_ref[...])
    xf_ref[...] = y
    xb_ref[...] = y.astype(BF16)


def _res_ln(x, y, g, b):
    T, D = x.shape
    tm = _tile(T, 512)
    row = pl.BlockSpec((tm, D), lambda i: (i, 0))
    vec = pl.BlockSpec((1, D), lambda i: (0, 0))
    return pl.pallas_call(
        _res_ln_kernel,
        grid=(T // tm,),
        in_specs=[row, row, vec, vec],
        out_specs=[row, row],
        out_shape=[jax.ShapeDtypeStruct((T, D), F32), jax.ShapeDtypeStruct((T, D), BF16)],
        compiler_params=_cparams("parallel"),
        name="res_ln",
    )(x, y, g.reshape(1, D), b.reshape(1, D))


def _combine_ln_kernel(x_ref, ya_ref, yb_ref, w_ref, g_ref, b_ref, xf_ref, xb_ref):
    w = w_ref[...]
    mix = w[:, 0:1] * ya_ref[...].astype(F32) + w[:, 1:2] * yb_ref[...].astype(F32)
    y = _ln_rows(ALPHA * x_ref[...] + mix, g_ref[...], b_ref[...])
    xf_ref[...] = y
    xb_ref[...] = y.astype(BF16)


def _combine_ln(x, ya, yb, w, g, b):
    T, D = x.shape
    tm = _tile(T, 512)
    row = pl.BlockSpec((tm, D), lambda i: (i, 0))
    wsp = pl.BlockSpec((tm, LANES), lambda i: (i, 0))
    vec = pl.BlockSpec((1, D), lambda i: (0, 0))
    return pl.pallas_call(
        _combine_ln_kernel,
        grid=(T // tm,),
        in_specs=[row, row, row, wsp, vec, vec],
        out_specs=[row, row],
        out_shape=[jax.ShapeDtypeStruct((T, D), F32), jax.ShapeDtypeStruct((T, D), BF16)],
        compiler_params=_cparams("parallel"),
        name="combine_ln",
    )(x, ya, yb, w, g.reshape(1, D), b.reshape(1, D))


def _mm_kernel(a_ref, w_ref, o_ref):
    o_ref[...] = jnp.dot(a_ref[...], w_ref[...], preferred_element_type=F32).astype(o_ref.dtype)


def _mm(a, w, out_dtype, tm_pref=1024, tn_pref=1024):
    M, K = a.shape
    N = w.shape[1]
    tm = _tile(M, tm_pref)
    tn = _tile(N, tn_pref, LANES)
    return pl.pallas_call(
        _mm_kernel,
        grid=(M // tm, N // tn),
        in_specs=[pl.BlockSpec((tm, K), lambda i, j: (i, 0)), pl.BlockSpec((K, tn), lambda i, j: (0, j))],
        out_specs=pl.BlockSpec((tm, tn), lambda i, j: (i, j)),
        out_shape=jax.ShapeDtypeStruct((M, N), out_dtype),
        compiler_params=_cparams("parallel", "arbitrary"),
        name="mm",
    )(a, w)


def _gate_mm_kernel(a_ref, w_ref, g_ref, o_ref):
    y = jnp.dot(a_ref[...], w_ref[...], preferred_element_type=F32)
    o_ref[...] = (_sigmoid(g_ref[...].astype(F32)) * y).astype(o_ref.dtype)


def _gate_mm(a, w, proj, gate_off):
    M, K = a.shape
    N = w.shape[1]
    tm = _tile(M, 1024)
    tn = _tile(N, 1024, LANES, divides=(gate_off,))
    goff = gate_off // tn
    return pl.pallas_call(
        _gate_mm_kernel,
        grid=(M // tm, N // tn),
        in_specs=[pl.BlockSpec((tm, K), lambda i, j: (i, 0)),
                  pl.BlockSpec((K, tn), lambda i, j: (0, j)),
                  pl.BlockSpec((tm, tn), lambda i, j: (i, goff + j))],
        out_specs=pl.BlockSpec((tm, tn), lambda i, j: (i, j)),
        out_shape=jax.ShapeDtypeStruct((M, N), BF16),
        compiler_params=_cparams("parallel", "arbitrary"),
        name="gate_mm",
    )(a, w, proj)


def _mm_res_ln_kernel(a_ref, w_ref, x_ref, g_ref, b_ref, xf_ref, xb_ref):
    y = jnp.dot(a_ref[...], w_ref[...], preferred_element_type=F32)
    out = _ln_rows(ALPHA * x_ref[...] + y, g_ref[...], b_ref[...])
    xf_ref[...] = out
    xb_ref[...] = out.astype(BF16)


def _mm_res_ln(a, w, x, g, b):
    M, K = a.shape
    D = w.shape[1]
    tm = _tile(M, 512)
    row = pl.BlockSpec((tm, D), lambda i: (i, 0))
    vec = pl.BlockSpec((1, D), lambda i: (0, 0))
    return pl.pallas_call(
        _mm_res_ln_kernel,
        grid=(M // tm,),
        in_specs=[pl.BlockSpec((tm, K), lambda i: (i, 0)), pl.BlockSpec((K, D), lambda i: (0, 0)), row, vec, vec],
        out_specs=[row, row],
        out_shape=[jax.ShapeDtypeStruct((M, D), F32), jax.ShapeDtypeStruct((M, D), BF16)],
        compiler_params=_cparams("parallel"),
        name="mm_res_ln",
    )(a, w, x, g.reshape(1, D), b.reshape(1, D))


PACK_DT = 0
PACK_STATE = DT_ROWS
PACK_OFF = 2 * DT_ROWS
PACK_E = 3 * DT_ROWS


def _conv_silu_chunk(src_ref, w, bias, c, nc, L):
    halo = BF16_SUBLANES
    n_ext = CHUNK + 2 * halo
    W = src_ref.shape[1]
    s0 = pl.multiple_of(c * CHUNK, CHUNK)
    sp = pl.multiple_of(jnp.maximum(s0 - halo, 0), halo)
    sn = pl.multiple_of(jnp.minimum(s0 + CHUNK, L - halo), halo)
    cur = src_ref[pl.ds(s0, CHUNK), :].astype(F32)
    prev = jnp.where(c > 0, src_ref[pl.ds(sp, halo), :].astype(F32), 0.0)
    nxt = jnp.where(c < nc - 1, src_ref[pl.ds(sn, halo), :].astype(F32), 0.0)
    ext = jnp.concatenate([prev, cur, nxt], axis=0)
    acc = jnp.broadcast_to(bias, (CHUNK, W))
    for k in range(CONV_K):
        shift = (n_ext - halo - (k - CONV_K // 2)) % n_ext
        acc = acc + w[k:k + 1, :] * pltpu.roll(ext, shift, 0)[:CHUNK]
    return _silu(acc)


def _ssd_kernel(z_ref, xs_ref, b_ref, c_ref, dt_ref, cwx_ref, cwb_ref, cwc_ref, cbx_ref, cbb_ref, cbc_ref,
                dtb_ref, alc_ref, all_ref, dsk_ref, gnw_ref, o_ref,
                xs_s, c_s, bt_s, cb_s, pack_s, erow_s, yb_s, stf_s, stb_s, *, L, P):
    nc = L // CHUNK
    half = nc // 2
    gw = xs_s.shape[1]

    ri = lax.broadcasted_iota(jnp.int32, (CHUNK, CHUNK), 0)
    ci = lax.broadcasted_iota(jnp.int32, (CHUNK, CHUNK), 1)
    incl_upper = (ri <= ci).astype(BF16)
    row16 = lax.broadcasted_iota(jnp.int32, (DT_ROWS, CHUNK), 0)
    zpad = jnp.zeros((CHUNK - DT_ROWS, CHUNK), F32)
    a_c = -jnp.exp(alc_ref[...])
    a_l = -jnp.exp(all_ref[...])
    dt_bias = dtb_ref[...]
    wx, wb, wc = cwx_ref[...], cwb_ref[...], cwc_ref[...]
    bx, bb, bc_bias = cbx_ref[...], cbb_ref[...], cbc_ref[...]

    def prepass(c, carry):
        s0 = pl.multiple_of(c * CHUNK, CHUNK)
        xs_s[pl.ds(s0, CHUNK), :] = _conv_silu_chunk(xs_ref, wx, bx, c, nc, L).astype(BF16)
        bcv = _conv_silu_chunk(b_ref, wb, bb, c, nc, L).astype(BF16)
        ccv = _conv_silu_chunk(c_ref, wc, bc_bias, c, nc, L).astype(BF16)
        c_s[pl.ds(s0, CHUNK), :] = ccv
        bt_s[:, pl.ds(s0, CHUNK)] = bcv.astype(F32).T.astype(BF16)
        cb = lax.dot_general(ccv, bcv, (((1,), (1,)), ((), ())), preferred_element_type=F32)
        cb_s[pl.ds(s0, CHUNK), :] = cb.astype(BF16)
        raw = dt_ref[:, pl.ds(s0, CHUNK)] + dt_bias
        dtv = jnp.maximum(raw, 0.0) + jnp.log1p(jnp.exp(-jnp.abs(raw)))
        a_row = dtv * a_c
        hi, mid, lo = _split3(a_row)
        acum_row = (jnp.dot(hi, incl_upper, preferred_element_type=F32)
                    + jnp.dot(mid, incl_upper, preferred_element_type=F32)
                    + jnp.dot(lo, incl_upper, preferred_element_type=F32))
        erow_s[:, pl.ds(s0, CHUNK)] = jnp.where(row16 < BWD_ROW0, acum_row, acum_row - a_row)
        acum_col = jnp.concatenate([acum_row, zpad], axis=0).T
        dt_col = jnp.concatenate([dtv, zpad], axis=0).T
        total = acum_col[CHUNK - 1:CHUNK, :]
        fwd_lane = ci < BWD_ROW0
        e_bwd = acum_col - dt_col * a_l
        e_col = jnp.where(fwd_lane, acum_col, e_bwd)
        f_state = dt_col * jnp.exp(jnp.where(fwd_lane, total - acum_col, e_bwd))
        f_off = jnp.exp(jnp.where(fwd_lane, acum_col, total - e_bwd))
        pack = jnp.where(ci < PACK_STATE, dt_col,
                         jnp.where(ci < PACK_OFF, pltpu.roll(f_state, PACK_STATE, 1),
                                   jnp.where(ci < PACK_E, pltpu.roll(f_off, PACK_OFF, 1),
                                             jnp.where(ci < PACK_E + DT_ROWS, pltpu.roll(e_col, PACK_E, 1), 0.0))))
        pack_s[pl.ds(s0, CHUNK), :] = pack
        return carry

    lax.fori_loop(0, nc, prepass, 0)

    def expand_matrix(row0):
        ej = lax.broadcasted_iota(jnp.int32, (CHUNK, gw), 0)
        ec = lax.broadcasted_iota(jnp.int32, (CHUNK, gw), 1)
        blocks = []
        for f in (PACK_DT, PACK_STATE, PACK_OFF):
            lo = (ej - f - row0) * P
            blocks.append(((ec >= lo) & (ec < lo + P)).astype(BF16))
        return jnp.concatenate(blocks, axis=1)

    expand_f = expand_matrix(0)
    expand_b = expand_matrix(BWD_ROW0)
    mask_f = ci <= ri
    mask_b = ci >= ri
    stf_s[...] = jnp.zeros_like(stf_s)
    stb_s[...] = jnp.zeros_like(stb_s)

    def scan_chunk(c, bwd):
        row0 = BWD_ROW0 if bwd else 0
        st_ref = stb_s if bwd else stf_s
        s0 = pl.multiple_of(c * CHUNK, CHUNK)
        pk = pack_s[pl.ds(s0, CHUNK), :]
        ex = jnp.dot(pk.astype(BF16), expand_b if bwd else expand_f, preferred_element_type=F32)
        dt_x = ex[:, :gw]
        fs_x = ex[:, gw:2 * gw]
        fo_x = ex[:, 2 * gw:]
        xc = xs_s[pl.ds(s0, CHUNK), :].astype(F32)
        x_dt = (xc * dt_x).astype(BF16)
        x_st = (xc * fs_x).astype(BF16)
        cc = c_s[pl.ds(s0, CHUNK), :]
        cbm = jnp.where(mask_b if bwd else mask_f, cb_s[pl.ds(s0, CHUNK), :].astype(F32), 0.0)
        state = st_ref[...]
        y = jnp.dot(cc, state.astype(BF16), preferred_element_type=F32) * fo_x
        er = erow_s[:, pl.ds(s0, CHUNK)]
        per_tile = LANES // P
        parts = []
        for q in range(gw // LANES):
            tile = x_dt[:, q * LANES:(q + 1) * LANES]
            acc = None
            for r in range(per_tile):
                j = row0 + q * per_tile + r
                col = pk[:, PACK_E + j:PACK_E + j + 1]
                row = er[j:j + 1, :]
                seg = (row - col) if bwd else (col - row)
                lm = (jnp.exp(jnp.minimum(seg, 0.0)) * cbm).astype(BF16)
                rhs = jnp.where((ci >= r * P) & (ci < (r + 1) * P), tile, jnp.zeros_like(tile))
                d = jnp.dot(lm, rhs, preferred_element_type=F32)
                acc = d if acc is None else acc + d
            parts.append(acc)
        y = y + jnp.concatenate(parts, axis=1)
        e_total = fo_x[0:1, :] if bwd else fo_x[CHUNK - 1:CHUNK, :]
        st_ref[...] = state * e_total + jnp.dot(bt_s[:, pl.ds(s0, CHUNK)], x_st, preferred_element_type=F32)
        return y, xc

    def finish(y, xc, c):
        s0 = pl.multiple_of(c * CHUNK, CHUNK)
        v = (y + xc * dsk_ref[...]) * _silu(z_ref[pl.ds(s0, CHUNK), :].astype(F32))
        ms = jnp.mean(v * v, axis=-1, keepdims=True)
        o_ref[pl.ds(s0, CHUNK), :] = (v * lax.rsqrt(ms + RMS_EPS) * gnw_ref[...]).astype(o_ref.dtype)

    def first_half(i, carry):
        cf = i
        cb_ = nc - 1 - i
        yf, _ = scan_chunk(cf, False)
        yb, _ = scan_chunk(cb_, True)
        o_ref[pl.ds(pl.multiple_of(cf * CHUNK, CHUNK), CHUNK), :] = yf.astype(o_ref.dtype)
        yb_s[pl.ds(pl.multiple_of((cb_ - half) * CHUNK, CHUNK), CHUNK), :] = yb.astype(yb_s.dtype)
        return carry

    def second_half(i, carry):
        cf = i
        cb_ = nc - 1 - i
        yf, xf = scan_chunk(cf, False)
        yb, xb = scan_chunk(cb_, True)
        yb_prev = yb_s[pl.ds(pl.multiple_of((cf - half) * CHUNK, CHUNK), CHUNK), :].astype(F32)
        yf_prev = o_ref[pl.ds(pl.multiple_of(cb_ * CHUNK, CHUNK), CHUNK), :].astype(F32)
        finish(yf + yb_prev, xf, cf)
        finish(yb + yf_prev, xb, cb_)
        return carry

    lax.fori_loop(0, half, first_half, 0)
    lax.fori_loop(half, nc, second_half, 0)


def _ssd(proj3, dt4, conv_w, conv_b, dt_bias_c, a_log_c, a_log_l, d_skip_x, gnorm_w, DI, P):
    B, L, _ = proj3.shape
    G = SSD_GROUPS
    gw = DI // G
    assert D_STATE == LANES and L % (2 * CHUNK) == 0 and gw % LANES == 0 and LANES % P == 0 and gw // P <= BWD_ROW0
    xoff = DI // gw
    boff = 2 * DI // LANES
    coff = boff + G
    cwb_off = DI // LANES
    io_bytes = L * (3 * gw + 2 * D_STATE) * 2
    scratch_bytes = L * (gw + 3 * D_STATE) * 2 + L * LANES * 4 + (L // 2) * gw * 2
    single = 2 * io_bytes + scratch_bytes > (VMEM_LIMIT_BYTES * 3) // 4

    def big(shape, imap):
        if single:
            return pl.BlockSpec(shape, imap, pipeline_mode=pl.Buffered(1))
        return pl.BlockSpec(shape, imap)

    in_specs = [
        big((None, L, gw), lambda b, g: (b, 0, g)),
        big((None, L, gw), lambda b, g: (b, 0, xoff + g)),
        big((None, L, D_STATE), lambda b, g: (b, 0, boff + g)),
        big((None, L, D_STATE), lambda b, g: (b, 0, coff + g)),
        pl.BlockSpec((None, None, DT_ROWS, L), lambda b, g: (b, g, 0, 0)),
        pl.BlockSpec((CONV_K, gw), lambda b, g: (0, g)),
        pl.BlockSpec((CONV_K, D_STATE), lambda b, g: (0, cwb_off + g)),
        pl.BlockSpec((CONV_K, D_STATE), lambda b, g: (0, cwb_off + G + g)),
        pl.BlockSpec((1, gw), lambda b, g: (0, g)),
        pl.BlockSpec((1, D_STATE), lambda b, g: (0, cwb_off + g)),
        pl.BlockSpec((1, D_STATE), lambda b, g: (0, cwb_off + G + g)),
        pl.BlockSpec((None, DT_ROWS, 1), lambda b, g: (g, 0, 0)),
        pl.BlockSpec((None, DT_ROWS, 1), lambda b, g: (g, 0, 0)),
        pl.BlockSpec((None, 1, LANES), lambda b, g: (g, 0, 0)),
        pl.BlockSpec((1, gw), lambda b, g: (0, g)),
        pl.BlockSpec((1, gw), lambda b, g: (0, g)),
    ]
    scratch = [
        pltpu.VMEM((L, gw), BF16),
        pltpu.VMEM((L, D_STATE), BF16),
        pltpu.VMEM((D_STATE, L), BF16),
        pltpu.VMEM((L, CHUNK), BF16),
        pltpu.VMEM((L, LANES), F32),
        pltpu.VMEM((DT_ROWS, L), F32),
        pltpu.VMEM((L // 2, gw), BF16),
        pltpu.VMEM((D_STATE, gw), F32),
        pltpu.VMEM((D_STATE, gw), F32),
    ]
    return pl.pallas_call(
        functools.partial(_ssd_kernel, L=L, P=P),
        grid=(B, G),
        in_specs=in_specs,
        out_specs=big((None, L, gw), lambda b, g: (b, 0, g)),
        out_shape=jax.ShapeDtypeStruct((B, L, DI), BF16),
        scratch_shapes=scratch,
        compiler_params=_cparams("parallel", "parallel"),
        name="ssd",
    )(proj3, proj3, proj3, proj3, dt4, conv_w, conv_w, conv_w, conv_b, conv_b, conv_b,
      dt_bias_c, a_log_c, a_log_l, d_skip_x, gnorm_w)


def _fft1_kernel(w_ref, tw_ref, u_ref, o_ref, *, L1, nl2, C):
    res = jnp.dot(w_ref[...], u_ref[...], preferred_element_type=F32)
    rep = C // LANES
    for q in range(nl2):
        a = res[:L1, q * C:(q + 1) * C]
        b = res[L1:, q * C:(q + 1) * C]
        twr = jnp.concatenate([tw_ref[0, q]] * rep, axis=1)
        twi = jnp.concatenate([tw_ref[1, q]] * rep, axis=1)
        o_ref[0, q] = (a * twr - b * twi).astype(o_ref.dtype)
        o_ref[1, q] = (a * twi + b * twr).astype(o_ref.dtype)


def _fft_stage1(u3, L1):
    B, L, C = u3.shape
    L2 = L // L1
    nl2 = 8
    k1 = np.arange(L1)
    ang1 = 2.0 * np.pi * np.outer(k1, np.arange(L1)) / L1
    w1 = jnp.asarray(np.concatenate([np.cos(ang1), -np.sin(ang1)], axis=0), F32).astype(BF16)
    angt = 2.0 * np.pi * np.outer(np.arange(L2), k1) / L
    tw = np.stack([np.cos(angt), -np.sin(angt)], axis=0)[..., None]
    tw = jnp.asarray(np.broadcast_to(tw, (2, L2, L1, LANES)), F32)
    uv = u3.reshape(B, L1, L2 * C)
    return pl.pallas_call(
        functools.partial(_fft1_kernel, L1=L1, nl2=nl2, C=C),
        grid=(B, L2 // nl2),
        in_specs=[pl.BlockSpec((2 * L1, L1), lambda b, j: (0, 0)),
                  pl.BlockSpec((2, nl2, L1, LANES), lambda b, j: (0, j, 0, 0)),
                  pl.BlockSpec((None, L1, nl2 * C), lambda b, j: (b, 0, j))],
        out_specs=pl.BlockSpec((None, 2, nl2, L1, C), lambda b, j: (b, 0, j, 0, 0)),
        out_shape=jax.ShapeDtypeStruct((B, 2, L2, L1, C), BF16),
        compiler_params=_cparams("parallel", "parallel"),
        name="fft_stage1",
    )(w1, tw, uv)


def _fft_stage2(g5):
    B, _, L2, L1, C = g5.shape
    ang = 2.0 * np.pi * np.outer(np.arange(L2), np.arange(L2)) / L2
    cs, sn = np.cos(ang), np.sin(ang)
    w2 = jnp.asarray(np.block([[cs, sn], [-sn, cs]]), F32).astype(BF16)
    N = L1 * C
    tn = _tile(N, 4096, LANES)
    gv = g5.reshape(B, 2 * L2, N)
    y = pl.pallas_call(
        _mm_kernel,
        grid=(B, N // tn),
        in_specs=[pl.BlockSpec((2 * L2, 2 * L2), lambda b, j: (0, 0)),
                  pl.BlockSpec((None, 2 * L2, tn), lambda b, j: (b, 0, j))],
        out_specs=pl.BlockSpec((None, 2 * L2, tn), lambda b, j: (b, 0, j)),
        out_shape=jax.ShapeDtypeStruct((B, 2 * L2, N), BF16),
        compiler_params=_cparams("parallel", "parallel"),
        name="fft_stage2",
    )(w2, gv)
    return y.reshape(B, 2, L2 * L1, C)


def _four_out_kernel(y_ref, cs_ref, wf_ref, g_ref, t_ref, o_ref, f_s, *, scale):
    @pl.when(pl.program_id(2) == 0)
    def _():
        yr = y_ref[0]
        yi = y_ref[1]
        cs = cs_ref[...]
        parts = []
        for g in range(FOUR_GROUPS):
            sl = slice(g * FOUR_GROUP_DIM, (g + 1) * FOUR_GROUP_DIM)
            lhs = jnp.concatenate([yr[:, sl], yi[:, sl]], axis=1)
            parts.append(jnp.dot(lhs, cs, preferred_element_type=F32))
        f_s[...] = (jnp.concatenate(parts, axis=1) * scale).astype(f_s.dtype)

    yf = jnp.dot(f_s[...], wf_ref[...], preferred_element_type=F32)
    o_ref[...] = (t_ref[...].astype(F32) + _sigmoid(g_ref[...].astype(F32)) * yf).astype(o_ref.dtype)


def _four_out(y4, w_four, proj, gate_off, t1):
    B, _, L, C = y4.shape
    D = w_four.shape[1]
    tm = _tile(L, 512)
    tn = _tile(D, 1024, LANES, divides=(gate_off,))
    goff = gate_off // tn
    nti = L // tm
    ang = 2.0 * np.pi * np.outer(np.arange(FOUR_GROUP_DIM), np.arange(FOUR_GROUP_DIM)) / FOUR_GROUP_DIM
    cs = jnp.asarray(np.concatenate([np.cos(ang), np.sin(ang)], axis=0), F32).astype(BF16)
    scale = 1.0 / math.sqrt(L * FOUR_GROUP_DIM)
    return pl.pallas_call(
        functools.partial(_four_out_kernel, scale=scale),
        grid=(B, nti, D // tn),
        in_specs=[pl.BlockSpec((None, 2, tm, C), lambda b, i, j: (b, 0, i, 0)),
                  pl.BlockSpec((2 * FOUR_GROUP_DIM, FOUR_GROUP_DIM), lambda b, i, j: (0, 0)),
                  pl.BlockSpec((C, tn), lambda b, i, j: (0, j)),
                  pl.BlockSpec((tm, tn), lambda b, i, j: (b * nti + i, goff + j)),
                  pl.BlockSpec((tm, tn), lambda b, i, j: (b * nti + i, j))],
        out_specs=pl.BlockSpec((tm, tn), lambda b, i, j: (b * nti + i, j)),
        out_shape=jax.ShapeDtypeStruct((B * L, D), BF16),
        scratch_shapes=[pltpu.VMEM((tm, C), BF16)],
        compiler_params=_cparams("parallel", "parallel", "arbitrary"),
        name="four_out",
    )(y4, cs, w_four, proj, t1)


def _ffn_kernel(te_ref, act_ref, x_ref, w1_ref, w3_ref, w2_ref, o_ref, acc_s):
    i = pl.program_id(0)
    j = pl.program_id(1)
    nj = pl.num_programs(1)

    @pl.when(j == 0)
    def _():
        acc_s[...] = jnp.zeros_like(acc_s)

    @pl.when(act_ref[i] > 0)
    def _():
        x = x_ref[...]
        h1 = jnp.dot(x, w1_ref[...], preferred_element_type=F32)
        h3 = jnp.dot(x, w3_ref[...], preferred_element_type=F32)
        h = (_silu(h1) * h3).astype(BF16)
        acc_s[...] += jnp.dot(h, w2_ref[...], preferred_element_type=F32)

    @pl.when(j == nj - 1)
    def _():
        o_ref[...] = acc_s[...].astype(o_ref.dtype)


def _ffn(x, w1, w3, w2, tile_expert, tile_active, tm):
    R, D = x.shape
    E, _, F = w1.shape
    tf = _tile(F, 512, LANES)
    nf = F // tf

    def wcol(i, j, te, act):
        return (te[i], 0, jnp.where(act[i] > 0, j, nf - 1))

    def wrow(i, j, te, act):
        return (te[i], jnp.where(act[i] > 0, j, nf - 1), 0)

    grid_spec = pltpu.PrefetchScalarGridSpec(
        num_scalar_prefetch=2,
        grid=(R // tm, nf),
        in_specs=[pl.BlockSpec((tm, D), lambda i, j, te, act: (i, 0)),
                  pl.BlockSpec((None, D, tf), wcol),
                  pl.BlockSpec((None, D, tf), wcol),
                  pl.BlockSpec((None, tf, D), wrow)],
        out_specs=pl.BlockSpec((tm, D), lambda i, j, te, act: (i, 0)),
        scratch_shapes=[pltpu.VMEM((tm, D), F32)],
    )
    return pl.pallas_call(
        _ffn_kernel,
        grid_spec=grid_spec,
        out_shape=jax.ShapeDtypeStruct((R, D), BF16),
        compiler_params=_cparams("parallel", "arbitrary"),
        name="ffn",
    )(tile_expert, tile_active, x, w1, w3, w2)


def _router_kernel(x_ref, r_ref, w_ref, i_ref, *, E):
    xh, xm, _ = _split3(x_ref[...])
    rh, rm, _ = _split3(r_ref[...])
    logits = (jnp.dot(xh, rh, preferred_element_type=F32) + jnp.dot(xm, rh, preferred_element_type=F32)
              + jnp.dot(xh, rm, preferred_element_type=F32))
    lane = lax.broadcasted_iota(jnp.int32, logits.shape, 1).astype(F32)
    logits = jnp.where(lane < E, logits, -jnp.inf)
    ex = jnp.exp(logits - jnp.max(logits, axis=-1, keepdims=True))
    probs = ex / jnp.sum(ex, axis=-1, keepdims=True)
    m1 = jnp.max(probs, axis=-1, keepdims=True)
    i1 = jnp.min(jnp.where(probs == m1, lane, float(LANES)), axis=-1, keepdims=True)
    rest = jnp.where(lane == i1, -1.0, probs)
    m2 = jnp.max(rest, axis=-1, keepdims=True)
    i2 = jnp.min(jnp.where(rest == m2, lane, float(LANES)), axis=-1, keepdims=True)
    den = m1 + m2
    w_ref[...] = jnp.where(lane == 0.0, m1 / den, jnp.where(lane == 1.0, m2 / den, 0.0))
    i_ref[...] = jnp.where(lane == 0.0, i1, jnp.where(lane == 1.0, i2, 0.0)).astype(jnp.int32)


def _router(x, router):
    T, D = x.shape
    E = router.shape[1]
    tm = _tile(T, 512)
    rp = jnp.zeros((D, LANES), F32).at[:, :E].set(router)
    row = pl.BlockSpec((tm, LANES), lambda i: (i, 0))
    return pl.pallas_call(
        functools.partial(_router_kernel, E=E),
        grid=(T // tm,),
        in_specs=[pl.BlockSpec((tm, D), lambda i: (i, 0)), pl.BlockSpec((D, LANES), lambda i: (0, 0))],
        out_specs=[row, row],
        out_shape=[jax.ShapeDtypeStruct((T, LANES), F32), jax.ShapeDtypeStruct((T, LANES), jnp.int32)],
        compiler_params=_cparams("parallel"),
        name="router",
    )(x, rp)


def _moe(xf, xb, router, we1, we3, we2, ln_g, ln_b):
    T, D = xf.shape
    E = we1.shape[0]
    tm = _tile(T, 1024)
    wt, idx = _router(xf, router)
    flat_e = idx[:, :TOP_K].reshape(-1)
    onehot = (flat_e[:, None] == jnp.arange(E, dtype=jnp.int32)[None, :]).astype(jnp.int32)
    rank = jnp.sum((jnp.cumsum(onehot, axis=0) - onehot) * onehot, axis=1)
    counts = jnp.sum(onehot, axis=0)
    padded = ((counts + tm - 1) // tm) * tm
    ends = jnp.cumsum(padded)
    pos = (ends - padded)[flat_e] + rank
    n_tiles = (TOP_K * T) // tm + E
    src = jnp.zeros((n_tiles * tm,), jnp.int32).at[pos].set(
        jnp.arange(TOP_K * T, dtype=jnp.int32) // TOP_K, unique_indices=True, mode="promise_in_bounds")
    starts = jnp.arange(n_tiles, dtype=jnp.int32) * tm
    tile_expert = jnp.minimum(jnp.searchsorted(ends, starts, side="right"), E - 1).astype(jnp.int32)
    tile_active = (starts < ends[-1]).astype(jnp.int32)
    x_sorted = xb.at[src].get(mode="promise_in_bounds")
    y_sorted = _ffn(x_sorted, we1, we3, we2, tile_expert, tile_active, tm)
    pos2 = pos.reshape(T, TOP_K)
    ya = y_sorted.at[pos2[:, 0]].get(mode="promise_in_bounds", unique_indices=True)
    yb = y_sorted.at[pos2[:, 1]].get(mode="promise_in_bounds", unique_indices=True)
    return _combine_ln(xf, ya, yb, wt, ln_g, ln_b)


def _prep_mixer(w_in, conv_w, conv_b, dt_bias_f, dt_bias_b, a_log_f, a_log_b, d_skip, gnorm_w,
                w_ssd_up, w_four, w_o):
    D = w_in.shape[0]
    DI = w_ssd_up.shape[0]
    CD = conv_b.shape[0]
    H = dt_bias_f.shape[0]
    DF = w_four.shape[0]
    G = SSD_GROUPS
    hpg = H // G
    o_dtf = DI + CD
    o_dtb = o_dtf + H
    o_u = o_dtb + H
    o_g = o_u + DF
    w_main = jnp.concatenate([w_in[:, :o_dtf], w_in[:, o_g:]], axis=1).astype(BF16)
    w_u = w_in[:, o_u:o_g].astype(BF16)

    def dt_rows(f, b, fill=0.0):
        lead = f.shape[:-1]
        out = jnp.full(lead + (G, DT_ROWS), fill, F32)
        out = out.at[..., :hpg].set(f.reshape(lead + (G, hpg)))
        return out.at[..., BWD_ROW0:BWD_ROW0 + hpg].set(b.reshape(lead + (G, hpg)))

    w_dt = dt_rows(w_in[:, o_dtf:o_dtb], w_in[:, o_dtb:o_u]).reshape(D, G * DT_ROWS).astype(BF16)
    dt_bias = dt_rows(dt_bias_f, dt_bias_b)
    a_log = dt_rows(a_log_f, a_log_b)
    a_log_l = jnp.zeros((G, 1, LANES), F32).at[:, 0, :DT_ROWS].set(a_log)
    return dict(
        w_main=w_main, w_u=w_u, w_dt=w_dt,
        conv_w=conv_w.reshape(CONV_K, CD), conv_b=conv_b.reshape(1, CD),
        dt_bias_c=dt_bias[..., None], a_log_c=a_log[..., None], a_log_l=a_log_l,
        d_skip_x=jnp.repeat(d_skip, DI // H).reshape(1, DI), gnorm_w=gnorm_w.reshape(1, DI),
        w_ssd_up=w_ssd_up.astype(BF16), w_four=w_four.astype(BF16), w_o=w_o.astype(BF16),
        DI=DI, P=DI // H, gate_off=o_dtf, D=D)


def _token_mixer(xf, xb, B, L, mp, ln_g, ln_b):
    T = B * L
    G = SSD_GROUPS
    DI, D = mp["DI"], mp["D"]
    proj = _mm(xb, mp["w_main"], BF16)
    u = _mm(xb, mp["w_u"], BF16)
    dt = _mm(xb, mp["w_dt"], F32)
    dt4 = dt.reshape(B, L, G, DT_ROWS).transpose(0, 2, 3, 1)
    v = _ssd(proj.reshape(B, L, -1), dt4, mp["conv_w"], mp["conv_b"], mp["dt_bias_c"], mp["a_log_c"],
             mp["a_log_l"], mp["d_skip_x"], mp["gnorm_w"], DI, mp["P"])
    t1 = _gate_mm(v.reshape(T, DI), mp["w_ssd_up"], proj, mp["gate_off"])
    g5 = _fft_stage1(u.reshape(B, L, -1), L // FFT_L2)
    y4 = _fft_stage2(g5)
    merged = _four_out(y4, mp["w_four"], proj, mp["gate_off"] + D, t1)
    return _mm_res_ln(merged, mp["w_o"], xf, ln_g, ln_b)


def kernel(x_prompt, x_sample, ln_in_g, ln_in_b, w_in_0, conv_w_0, conv_b_0, dt_bias_f_0, dt_bias_b_0, a_log_f_0, a_log_b_0, d_skip_0, gnorm_w_0, w_ssd_up_0, w_four_0, w_o_0, ln1_g_0, ln1_b_0, w1_0, w3_0, w2_0, ln2_g_0, ln2_b_0, w_in_1, conv_w_1, conv_b_1, dt_bias_f_1, dt_bias_b_1, a_log_f_1, a_log_b_1, d_skip_1, gnorm_w_1, w_ssd_up_1, w_four_1, w_o_1, ln1_g_1, ln1_b_1, router_1, we1_1, we3_1, we2_1, ln2_g_1, ln2_b_1):
    mp0 = _prep_mixer(w_in_0, conv_w_0, conv_b_0, dt_bias_f_0, dt_bias_b_0, a_log_f_0, a_log_b_0,
                      d_skip_0, gnorm_w_0, w_ssd_up_0, w_four_0, w_o_0)
    mp1 = _prep_mixer(w_in_1, conv_w_1, conv_b_1, dt_bias_f_1, dt_bias_b_1, a_log_f_1, a_log_b_1,
                      d_skip_1, gnorm_w_1, w_ssd_up_1, w_four_1, w_o_1)
    w1 = w1_0.astype(BF16)[None]
    w3 = w3_0.astype(BF16)[None]
    w2 = w2_0.astype(BF16)[None]
    we1 = we1_1.astype(BF16)
    we3 = we3_1.astype(BF16)
    we2 = we2_1.astype(BF16)

    def trunk(x):
        B, L, D = x.shape
        T = B * L
        xf, xb = _ln_in(x.reshape(T, D), ln_in_g, ln_in_b)
        xf, xb = _token_mixer(xf, xb, B, L, mp0, ln1_g_0, ln1_b_0)
        tm = _tile(T, 1024)
        n_tiles = T // tm
        ffn = _ffn(xb, w1, w3, w2, jnp.zeros((n_tiles,), jnp.int32), jnp.ones((n_tiles,), jnp.int32), tm)
        xf, xb = _res_ln(xf, ffn, ln2_g_0, ln2_b_0)
        xf, xb = _token_mixer(xf, xb, B, L, mp1, ln1_g_1, ln1_b_1)
        xf, _ = _moe(xf, xb, router_1, we1, we3, we2, ln2_g_1, ln2_b_1)
        return xf.reshape(B, L, D)

    return (trunk(x_prompt), trunk(x_sample))
```

```python
import functools
import math

import numpy as np
import jax
import jax.numpy as jnp
from jax import lax
from jax.experimental import pallas as pl
from jax.experimental.pallas import tpu as pltpu

F32 = jnp.float32
BF16 = jnp.bfloat16

SSD_GROUPS = 8
D_STATE = 128
CONV_K = 5
CHUNK = 128
FOUR_GROUPS = 8
FOUR_GROUP_DIM = 128
TOP_K = 2
DEPTH = 2
ALPHA = (2 * DEPTH) ** 0.25
LN_EPS = 1e-5
RMS_EPS = 1e-5

LANES = 128
BF16_SUBLANES = 16
VMEM_LIMIT_BYTES = 56 * 1024 * 1024

DT_ROWS = 16
BWD_ROW0 = 8
FFT_L2 = 128
DENSE_DFT_MAX_L = 2048


def _cparams(*sem):
    return pltpu.CompilerParams(dimension_semantics=sem, vmem_limit_bytes=VMEM_LIMIT_BYTES)


def _tile(n, pref, align=8, divides=()):
    t = min(pref, n)
    t -= t % align
    while t > align:
        if n % t == 0 and all(d % t == 0 for d in divides):
            return t
        t -= align
    return align


def _sigmoid(x):
    return 1.0 / (1.0 + jnp.exp(-x))


def _silu(x):
    return x / (1.0 + jnp.exp(-x))


def _ln_rows(v, g, b):
    mu = jnp.mean(v, axis=-1, keepdims=True)
    d = v - mu
    var = jnp.mean(d * d, axis=-1, keepdims=True)
    return d * lax.rsqrt(var + LN_EPS) * g + b


def _split3(x):
    hi = x.astype(BF16)
    r1 = x - hi.astype(F32)
    mid = r1.astype(BF16)
    lo = (r1 - mid.astype(F32)).astype(BF16)
    return hi, mid, lo


def _ln_in_kernel(x_ref, g_ref, b_ref, xf_ref, xb_ref):
    y = _ln_rows(x_ref[...], g_ref[...], b_ref[...])
    xf_ref[...] = y
    xb_ref[...] = y.astype(BF16)


def _ln_in(x, g, b):
    T, D = x.shape
    tm = _tile(T, 512)
    row = pl.BlockSpec((tm, D), lambda i: (i, 0))
    vec = pl.BlockSpec((1, D), lambda i: (0, 0))
    return pl.pallas_call(
        _ln_in_kernel,
        grid=(T // tm,),
        in_specs=[row, vec, vec],
        out_specs=[row, row],
        out_shape=[jax.ShapeDtypeStruct((T, D), F32), jax.ShapeDtypeStruct((T, D), BF16)],
        compiler_params=_cparams("parallel"),
        name="ln_in",
    )(x, g.reshape(1, D), b.reshape(1, D))


def _res_ln_kernel(x_ref, y_ref, g_ref, b_ref, xf_ref, xb_ref):
    v = ALPHA * x_ref[...] + y_ref[...].astype(F32)
    y = _ln_rows(v, g_ref[...], b_ref[...])
    xf_ref[...] = y
    xb_ref[...] = y.astype(BF16)


def _res_ln(x, y, g, b):
    T, D = x.shape
    tm = _tile(T, 512)
    row = pl.BlockSpec((tm, D), lambda i: (i, 0))
    vec = pl.BlockSpec((1, D), lambda i: (0, 0))
    return pl.pallas_call(
        _res_ln_kernel,
        grid=(T // tm,),
        in_specs=[row, row, vec, vec],
        out_specs=[row, row],
        out_shape=[jax.ShapeDtypeStruct((T, D), F32), jax.ShapeDtypeStruct((T, D), BF16)],
        compiler_params=_cparams("parallel"),
        name="res_ln",
    )(x, y, g.reshape(1, D), b.reshape(1, D))


def _combine_ln_kernel(x_ref, ya_ref, yb_ref, w_ref, g_ref, b_ref, xf_ref, xb_ref):
    w = w_ref[...]
    mix = w[:, 0:1] * ya_ref[...].astype(F32) + w[:, 1:2] * yb_ref[...].astype(F32)
    y = _ln_rows(ALPHA * x_ref[...] + mix, g_ref[...], b_ref[...])
    xf_ref[...] = y
    xb_ref[...] = y.astype(BF16)


def _combine_ln(x, ya, yb, w, g, b):
    T, D = x.shape
    tm = _tile(T, 512)
    row = pl.BlockSpec((tm, D), lambda i: (i, 0))
    wsp = pl.BlockSpec((tm, LANES), lambda i: (i, 0))
    vec = pl.BlockSpec((1, D), lambda i: (0, 0))
    return pl.pallas_call(
        _combine_ln_kernel,
        grid=(T // tm,),
        in_specs=[row, row, row, wsp, vec, vec],
        out_specs=[row, row],
        out_shape=[jax.ShapeDtypeStruct((T, D), F32), jax.ShapeDtypeStruct((T, D), BF16)],
        compiler_params=_cparams("parallel"),
        name="combine_ln",
    )(x, ya, yb, w, g.reshape(1, D), b.reshape(1, D))


def _mm_kernel(a_ref, w_ref, o_ref):
    o_ref[...] = jnp.dot(a_ref[...], w_ref[...], preferred_element_type=F32).astype(o_ref.dtype)


def _mm(a, w, out_dtype, tm_pref=1024, tn_pref=1024):
    M, K = a.shape
    N = w.shape[1]
    tm = _tile(M, tm_pref)
    tn = _tile(N, tn_pref, LANES)
    return pl.pallas_call(
        _mm_kernel,
        grid=(M // tm, N // tn),
        in_specs=[pl.BlockSpec((tm, K), lambda i, j: (i, 0)), pl.BlockSpec((K, tn), lambda i, j: (0, j))],
        out_specs=pl.BlockSpec((tm, tn), lambda i, j: (i, j)),
        out_shape=jax.ShapeDtypeStruct((M, N), out_dtype),
        compiler_params=_cparams("parallel", "arbitrary"),
        name="mm",
    )(a, w)


def _gate_mm_kernel(a_ref, w_ref, g_ref, o_ref):
    y = jnp.dot(a_ref[...], w_ref[...], preferred_element_type=F32)
    o_ref[...] = (_sigmoid(g_ref[...].astype(F32)) * y).astype(o_ref.dtype)


def _gate_mm(a, w, proj, gate_off):
    M, K = a.shape
    N = w.shape[1]
    tm = _tile(M, 1024)
    tn = _tile(N, 1024, LANES, divides=(gate_off,))
    goff = gate_off // tn
    return pl.pallas_call(
        _gate_mm_kernel,
        grid=(M // tm, N // tn),
        in_specs=[pl.BlockSpec((tm, K), lambda i, j: (i, 0)),
                  pl.BlockSpec((K, tn), lambda i, j: (0, j)),
                  pl.BlockSpec((tm, tn), lambda i, j: (i, goff + j))],
        out_specs=pl.BlockSpec((tm, tn), lambda i, j: (i, j)),
        out_shape=jax.ShapeDtypeStruct((M, N), BF16),
        compiler_params=_cparams("parallel", "arbitrary"),
        name="gate_mm",
    )(a, w, proj)


def _mm_res_ln_kernel(a_ref, w_ref, x_ref, g_ref, b_ref, xf_ref, xb_ref):
    y = jnp.dot(a_ref[...], w_ref[...], preferred_element_type=F32)
    out = _ln_rows(ALPHA * x_ref[...] + y, g_ref[...], b_ref[...])
    xf_ref[...] = out
    xb_ref[...] = out.astype(BF16)


def _mm_res_ln(a, w, x, g, b):
    M, K = a.shape
    D = w.shape[1]
    tm = _tile(M, 512)
    row = pl.BlockSpec((tm, D), lambda i: (i, 0))
    vec = pl.BlockSpec((1, D), lambda i: (0, 0))
    return pl.pallas_call(
        _mm_res_ln_kernel,
        grid=(M // tm,),
        in_specs=[pl.BlockSpec((tm, K), lambda i: (i, 0)), pl.BlockSpec((K, D), lambda i: (0, 0)), row, vec, vec],
        out_specs=[row, row],
        out_shape=[jax.ShapeDtypeStruct((M, D), F32), jax.ShapeDtypeStruct((M, D), BF16)],
        compiler_params=_cparams("parallel"),
        name="mm_res_ln",
    )(a, w, x, g.reshape(1, D), b.reshape(1, D))


PACK_DT = 0
PACK_STATE = DT_ROWS
PACK_OFF = 2 * DT_ROWS
PACK_E = 3 * DT_ROWS


def _conv_silu_chunk(src_ref, w, bias, c, nc, L):
    halo = BF16_SUBLANES
    n_ext = CHUNK + 2 * halo
    W = src_ref.shape[1]
    s0 = pl.multiple_of(c * CHUNK, CHUNK)
    sp = pl.multiple_of(jnp.maximum(s0 - halo, 0), halo)
    sn = pl.multiple_of(jnp.minimum(s0 + CHUNK, L - halo), halo)
    cur = src_ref[pl.ds(s0, CHUNK), :].astype(F32)
    prev = jnp.where(c > 0, src_ref[pl.ds(sp, halo), :].astype(F32), 0.0)
    nxt = jnp.where(c < nc - 1, src_ref[pl.ds(sn, halo), :].astype(F32), 0.0)
    ext = jnp.concatenate([prev, cur, nxt], axis=0)
    acc = jnp.broadcast_to(bias, (CHUNK, W))
    for k in range(CONV_K):
        shift = (n_ext - halo - (k - CONV_K // 2)) % n_ext
        acc = acc + w[k:k + 1, :] * pltpu.roll(ext, shift, 0)[:CHUNK]
    return _silu(acc)


def _ssd_kernel(z_ref, xs_ref, b_ref, c_ref, dt_ref, cwx_ref, cwb_ref, cwc_ref, cbx_ref, cbb_ref, cbc_ref,
                dtb_ref, alc_ref, all_ref, dsk_ref, gnw_ref, o_ref,
                xs_s, c_s, bt_s, cb_s, pack_s, erow_s, yb_s, stf_s, stb_s, *, L, P):
    nc = L // CHUNK
    half = nc // 2
    gw = xs_s.shape[1]

    ri = lax.broadcasted_iota(jnp.int32, (CHUNK, CHUNK), 0)
    ci = lax.broadcasted_iota(jnp.int32, (CHUNK, CHUNK), 1)
    incl_upper = (ri <= ci).astype(BF16)
    row16 = lax.broadcasted_iota(jnp.int32, (DT_ROWS, CHUNK), 0)
    zpad = jnp.zeros((CHUNK - DT_ROWS, CHUNK), F32)
    a_c = -jnp.exp(alc_ref[...])
    a_l = -jnp.exp(all_ref[...])
    dt_bias = dtb_ref[...]
    wx, wb, wc = cwx_ref[...], cwb_ref[...], cwc_ref[...]
    bx, bb, bc_bias = cbx_ref[...], cbb_ref[...], cbc_ref[...]

    def prepass(c, carry):
        s0 = pl.multiple_of(c * CHUNK, CHUNK)
        xs_s[pl.ds(s0, CHUNK), :] = _conv_silu_chunk(xs_ref, wx, bx, c, nc, L).astype(BF16)
        bcv = _conv_silu_chunk(b_ref, wb, bb, c, nc, L).astype(BF16)
        ccv = _conv_silu_chunk(c_ref, wc, bc_bias, c, nc, L).astype(BF16)
        c_s[pl.ds(s0, CHUNK), :] = ccv
        bt_s[:, pl.ds(s0, CHUNK)] = bcv.astype(F32).T.astype(BF16)
        cb = lax.dot_general(ccv, bcv, (((1,), (1,)), ((), ())), preferred_element_type=F32)
        cb_s[pl.ds(s0, CHUNK), :] = cb.astype(BF16)
        raw = dt_ref[:, pl.ds(s0, CHUNK)] + dt_bias
        dtv = jnp.maximum(raw, 0.0) + jnp.log1p(jnp.exp(-jnp.abs(raw)))
        a_row = dtv * a_c
        hi, mid, lo = _split3(a_row)
        acum_row = (jnp.dot(hi, incl_upper, preferred_element_type=F32)
                    + jnp.dot(mid, incl_upper, preferred_element_type=F32)
                    + jnp.dot(lo, incl_upper, preferred_element_type=F32))
        erow_s[:, pl.ds(s0, CHUNK)] = jnp.where(row16 < BWD_ROW0, acum_row, acum_row - a_row)
        acum_col = jnp.concatenate([acum_row, zpad], axis=0).T
        dt_col = jnp.concatenate([dtv, zpad], axis=0).T
        total = acum_col[CHUNK - 1:CHUNK, :]
        fwd_lane = ci < BWD_ROW0
        e_bwd = acum_col - dt_col * a_l
        e_col = jnp.where(fwd_lane, acum_col, e_bwd)
        f_state = dt_col * jnp.exp(jnp.where(fwd_lane, total - acum_col, e_bwd))
        f_off = jnp.exp(jnp.where(fwd_lane, acum_col, total - e_bwd))
        pack = jnp.where(ci < PACK_STATE, dt_col,
                         jnp.where(ci < PACK_OFF, pltpu.roll(f_state, PACK_STATE, 1),
                                   jnp.where(ci < PACK_E, pltpu.roll(f_off, PACK_OFF, 1),
                                             jnp.where(ci < PACK_E + DT_ROWS, pltpu.roll(e_col, PACK_E, 1), 0.0))))
        pack_s[pl.ds(s0, CHUNK), :] = pack
        return carry

    prepass(0, 0)
    prepass(nc - 1, 0)

    def expand_matrix(row0):
        ej = lax.broadcasted_iota(jnp.int32, (CHUNK, gw), 0)
        ec = lax.broadcasted_iota(jnp.int32, (CHUNK, gw), 1)
        blocks = []
        for f in (PACK_DT, PACK_STATE, PACK_OFF):
            lo = (ej - f - row0) * P
            blocks.append(((ec >= lo) & (ec < lo + P)).astype(BF16))
        return jnp.concatenate(blocks, axis=1)

    expand_f = expand_matrix(0)
    expand_b = expand_matrix(BWD_ROW0)
    mask_f = ci <= ri
    mask_b = ci >= ri
    stf_s[...] = jnp.zeros_like(stf_s)
    stb_s[...] = jnp.zeros_like(stb_s)

    def scan_chunk(c, bwd):
        row0 = BWD_ROW0 if bwd else 0
        st_ref = stb_s if bwd else stf_s
        s0 = pl.multiple_of(c * CHUNK, CHUNK)
        pk = pack_s[pl.ds(s0, CHUNK), :]
        ex = jnp.dot(pk.astype(BF16), expand_b if bwd else expand_f, preferred_element_type=F32)
        dt_x = ex[:, :gw]
        fs_x = ex[:, gw:2 * gw]
        fo_x = ex[:, 2 * gw:]
        xc = xs_s[pl.ds(s0, CHUNK), :].astype(F32)
        x_dt = (xc * dt_x).astype(BF16)
        x_st = (xc * fs_x).astype(BF16)
        cc = c_s[pl.ds(s0, CHUNK), :]
        cbm = jnp.where(mask_b if bwd else mask_f, cb_s[pl.ds(s0, CHUNK), :].astype(F32), 0.0)
        state = st_ref[...]
        y = jnp.dot(cc, state.astype(BF16), preferred_element_type=F32) * fo_x
        er = erow_s[:, pl.ds(s0, CHUNK)]
        per_tile = LANES // P
        parts = []
        for q in range(gw // LANES):
            tile = x_dt[:, q * LANES:(q + 1) * LANES]
            acc = None
            for r in range(per_tile):
                j = row0 + q * per_tile + r
                col = pk[:, PACK_E + j:PACK_E + j + 1]
                row = er[j:j + 1, :]
                seg = (row - col) if bwd else (col - row)
                lm = (jnp.exp(jnp.minimum(seg, 0.0)) * cbm).astype(BF16)
                rhs = jnp.where((ci >= r * P) & (ci < (r + 1) * P), tile, jnp.zeros_like(tile))
                d = jnp.dot(lm, rhs, preferred_element_type=F32)
                acc = d if acc is None else acc + d
            parts.append(acc)
        y = y + jnp.concatenate(parts, axis=1)
        e_total = fo_x[0:1, :] if bwd else fo_x[CHUNK - 1:CHUNK, :]
        st_ref[...] = state * e_total + jnp.dot(bt_s[:, pl.ds(s0, CHUNK)], x_st, preferred_element_type=F32)
        return y, xc

    def finish(y, xc, c):
        s0 = pl.multiple_of(c * CHUNK, CHUNK)
        v = (y + xc * dsk_ref[...]) * _silu(z_ref[pl.ds(s0, CHUNK), :].astype(F32))
        ms = jnp.mean(v * v, axis=-1, keepdims=True)
        o_ref[pl.ds(s0, CHUNK), :] = (v * lax.rsqrt(ms + RMS_EPS) * gnw_ref[...]).astype(o_ref.dtype)

    def first_half(i, carry, prepare_next=True):
        cf = i
        cb_ = nc - 1 - i
        yf, _ = scan_chunk(cf, False)
        yb, _ = scan_chunk(cb_, True)
        o_ref[pl.ds(pl.multiple_of(cf * CHUNK, CHUNK), CHUNK), :] = yf.astype(o_ref.dtype)
        yb_s[pl.ds(pl.multiple_of((cb_ - half) * CHUNK, CHUNK), CHUNK), :] = yb.astype(yb_s.dtype)
        if prepare_next:
            prepass(cf + 1, 0)
            prepass(cb_ - 1, 0)
        return carry

    def second_half(i, carry):
        cf = i
        cb_ = nc - 1 - i
        yf, xf = scan_chunk(cf, False)
        yb, xb = scan_chunk(cb_, True)
        yb_prev = yb_s[pl.ds(pl.multiple_of((cf - half) * CHUNK, CHUNK), CHUNK), :].astype(F32)
        yf_prev = o_ref[pl.ds(pl.multiple_of(cb_ * CHUNK, CHUNK), CHUNK), :].astype(F32)
        finish(yf + yb_prev, xf, cf)
        finish(yb + yf_prev, xb, cb_)
        return carry

    lax.fori_loop(0, half - 1, first_half, 0)
    first_half(half - 1, 0, prepare_next=False)
    lax.fori_loop(half, nc, second_half, 0)


def _ssd(proj3, dt4, conv_w, conv_b, dt_bias_c, a_log_c, a_log_l, d_skip_x, gnorm_w, DI, P):
    B, L, _ = proj3.shape
    G = SSD_GROUPS
    gw = DI // G
    assert D_STATE == LANES and L % (2 * CHUNK) == 0 and gw % LANES == 0 and LANES % P == 0 and gw // P <= BWD_ROW0
    xoff = DI // gw
    boff = 2 * DI // LANES
    coff = boff + G
    cwb_off = DI // LANES
    io_bytes = L * (3 * gw + 2 * D_STATE) * 2
    scratch_bytes = L * (gw + 3 * D_STATE) * 2 + L * LANES * 4 + (L // 2) * gw * 2
    single = 2 * io_bytes + scratch_bytes > (VMEM_LIMIT_BYTES * 3) // 4

    def big(shape, imap):
        if single:
            return pl.BlockSpec(shape, imap, pipeline_mode=pl.Buffered(1))
        return pl.BlockSpec(shape, imap)

    in_specs = [
        big((None, L, gw), lambda b, g: (b, 0, g)),
        big((None, L, gw), lambda b, g: (b, 0, xoff + g)),
        big((None, L, D_STATE), lambda b, g: (b, 0, boff + g)),
        big((None, L, D_STATE), lambda b, g: (b, 0, coff + g)),
        pl.BlockSpec((None, None, DT_ROWS, L), lambda b, g: (b, g, 0, 0)),
        pl.BlockSpec((CONV_K, gw), lambda b, g: (0, g)),
        pl.BlockSpec((CONV_K, D_STATE), lambda b, g: (0, cwb_off + g)),
        pl.BlockSpec((CONV_K, D_STATE), lambda b, g: (0, cwb_off + G + g)),
        pl.BlockSpec((1, gw), lambda b, g: (0, g)),
        pl.BlockSpec((1, D_STATE), lambda b, g: (0, cwb_off + g)),
        pl.BlockSpec((1, D_STATE), lambda b, g: (0, cwb_off + G + g)),
        pl.BlockSpec((None, DT_ROWS, 1), lambda b, g: (g, 0, 0)),
        pl.BlockSpec((None, DT_ROWS, 1), lambda b, g: (g, 0, 0)),
        pl.BlockSpec((None, 1, LANES), lambda b, g: (g, 0, 0)),
        pl.BlockSpec((1, gw), lambda b, g: (0, g)),
        pl.BlockSpec((1, gw), lambda b, g: (0, g)),
    ]
    scratch = [
        pltpu.VMEM((L, gw), BF16),
        pltpu.VMEM((L, D_STATE), BF16),
        pltpu.VMEM((D_STATE, L), BF16),
        pltpu.VMEM((L, CHUNK), BF16),
        pltpu.VMEM((L, LANES), F32),
        pltpu.VMEM((DT_ROWS, L), F32),
        pltpu.VMEM((L // 2, gw), BF16),
        pltpu.VMEM((D_STATE, gw), F32),
        pltpu.VMEM((D_STATE, gw), F32),
    ]
    return pl.pallas_call(
        functools.partial(_ssd_kernel, L=L, P=P),
        grid=(B, G),
        in_specs=in_specs,
        out_specs=big((None, L, gw), lambda b, g: (b, 0, g)),
        out_shape=jax.ShapeDtypeStruct((B, L, DI), BF16),
        scratch_shapes=scratch,
        compiler_params=_cparams("parallel", "parallel"),
        name="ssd",
    )(proj3, proj3, proj3, proj3, dt4, conv_w, conv_w, conv_w, conv_b, conv_b, conv_b,
      dt_bias_c, a_log_c, a_log_l, d_skip_x, gnorm_w)


def _fft1_kernel(w_ref, tw_ref, u_ref, o_ref, *, L1, nl2, C):
    res = jnp.dot(w_ref[...], u_ref[...], preferred_element_type=F32)
    rep = C // LANES
    for q in range(nl2):
        a = res[:L1, q * C:(q + 1) * C]
        b = res[L1:, q * C:(q + 1) * C]
        twr = jnp.concatenate([tw_ref[0, q]] * rep, axis=1)
        twi = jnp.concatenate([tw_ref[1, q]] * rep, axis=1)
        o_ref[0, q] = (a * twr - b * twi).astype(o_ref.dtype)
        o_ref[1, q] = (a * twi + b * twr).astype(o_ref.dtype)


def _fft_stage1(u3, L1):
    B, L, C = u3.shape
    L2 = L // L1
    nl2 = 8
    k1 = np.arange(L1)
    ang1 = 2.0 * np.pi * np.outer(k1, np.arange(L1)) / L1
    w1 = jnp.asarray(np.concatenate([np.cos(ang1), -np.sin(ang1)], axis=0), F32).astype(BF16)
    angt = 2.0 * np.pi * np.outer(np.arange(L2), k1) / L
    tw = np.stack([np.cos(angt), -np.sin(angt)], axis=0)[..., None]
    tw = jnp.asarray(np.broadcast_to(tw, (2, L2, L1, LANES)), F32)
    uv = u3.reshape(B, L1, L2 * C)
    return pl.pallas_call(
        functools.partial(_fft1_kernel, L1=L1, nl2=nl2, C=C),
        grid=(B, L2 // nl2),
        in_specs=[pl.BlockSpec((2 * L1, L1), lambda b, j: (0, 0)),
                  pl.BlockSpec((2, nl2, L1, LANES), lambda b, j: (0, j, 0, 0)),
                  pl.BlockSpec((None, L1, nl2 * C), lambda b, j: (b, 0, j))],
        out_specs=pl.BlockSpec((None, 2, nl2, L1, C), lambda b, j: (b, 0, j, 0, 0)),
        out_shape=jax.ShapeDtypeStruct((B, 2, L2, L1, C), BF16),
        compiler_params=_cparams("parallel", "parallel"),
        name="fft_stage1",
    )(w1, tw, uv)


def _fft_stage2(g5):
    B, _, L2, L1, C = g5.shape
    ang = 2.0 * np.pi * np.outer(np.arange(L2), np.arange(L2)) / L2
    cs, sn = np.cos(ang), np.sin(ang)
    w2 = jnp.asarray(np.block([[cs, sn], [-sn, cs]]), F32).astype(BF16)
    N = L1 * C
    tn = _tile(N, 4096, LANES)
    gv = g5.reshape(B, 2 * L2, N)
    y = pl.pallas_call(
        _mm_kernel,
        grid=(B, N // tn),
        in_specs=[pl.BlockSpec((2 * L2, 2 * L2), lambda b, j: (0, 0)),
                  pl.BlockSpec((None, 2 * L2, tn), lambda b, j: (b, 0, j))],
        out_specs=pl.BlockSpec((None, 2 * L2, tn), lambda b, j: (b, 0, j)),
        out_shape=jax.ShapeDtypeStruct((B, 2 * L2, N), BF16),
        compiler_params=_cparams("parallel", "parallel"),
        name="fft_stage2",
    )(w2, gv)
    return y.reshape(B, 2, L2 * L1, C)


def _four_out_kernel(y_ref, cs_ref, wf_ref, g_ref, t_ref, o_ref, f_s, *, scale):
    @pl.when(pl.program_id(2) == 0)
    def _():
        yr = y_ref[0]
        yi = y_ref[1]
        cs = cs_ref[...]
        parts = []
        for g in range(FOUR_GROUPS):
            sl = slice(g * FOUR_GROUP_DIM, (g + 1) * FOUR_GROUP_DIM)
            lhs = jnp.concatenate([yr[:, sl], yi[:, sl]], axis=1)
            parts.append(jnp.dot(lhs, cs, preferred_element_type=F32))
        f_s[...] = (jnp.concatenate(parts, axis=1) * scale).astype(f_s.dtype)

    yf = jnp.dot(f_s[...], wf_ref[...], preferred_element_type=F32)
    o_ref[...] = (t_ref[...].astype(F32) + _sigmoid(g_ref[...].astype(F32)) * yf).astype(o_ref.dtype)


def _four_out(y4, w_four, proj, gate_off, t1):
    B, _, L, C = y4.shape
    D = w_four.shape[1]
    tm = _tile(L, 1024)
    tn = _tile(D, 2048, LANES, divides=(gate_off,))
    goff = gate_off // tn
    nti = L // tm
    ang = 2.0 * np.pi * np.outer(np.arange(FOUR_GROUP_DIM), np.arange(FOUR_GROUP_DIM)) / FOUR_GROUP_DIM
    cs = jnp.asarray(np.concatenate([np.cos(ang), np.sin(ang)], axis=0), F32).astype(BF16)
    scale = 1.0 / math.sqrt(L * FOUR_GROUP_DIM)
    return pl.pallas_call(
        functools.partial(_four_out_kernel, scale=scale),
        grid=(B, nti, D // tn),
        in_specs=[pl.BlockSpec((None, 2, tm, C), lambda b, i, j: (b, 0, i, 0)),
                  pl.BlockSpec((2 * FOUR_GROUP_DIM, FOUR_GROUP_DIM), lambda b, i, j: (0, 0)),
                  pl.BlockSpec((C, tn), lambda b, i, j: (0, j)),
                  pl.BlockSpec((tm, tn), lambda b, i, j: (b * nti + i, goff + j)),
                  pl.BlockSpec((tm, tn), lambda b, i, j: (b * nti + i, j))],
        out_specs=pl.BlockSpec((tm, tn), lambda b, i, j: (b * nti + i, j)),
        out_shape=jax.ShapeDtypeStruct((B * L, D), BF16),
        scratch_shapes=[pltpu.VMEM((tm, C), BF16)],
        compiler_params=_cparams("parallel", "parallel", "arbitrary"),
        name="four_out",
    )(y4, cs, w_four, proj, t1)


def _chan_dft_kernel(u_ref, cs_ref, o_ref):
    u = u_ref[...]
    cs = cs_ref[...]
    cos_parts, sin_parts = [], []
    for g in range(FOUR_GROUPS):
        r = jnp.dot(u[:, g * FOUR_GROUP_DIM:(g + 1) * FOUR_GROUP_DIM], cs, preferred_element_type=F32)
        cos_parts.append(r[:, :FOUR_GROUP_DIM])
        sin_parts.append(r[:, FOUR_GROUP_DIM:])
    o_ref[0] = jnp.concatenate(cos_parts, axis=1).astype(o_ref.dtype)
    o_ref[1] = jnp.concatenate(sin_parts, axis=1).astype(o_ref.dtype)


def _chan_dft(u, B, L):
    T, C = u.shape
    tm = _tile(L, 1024)
    nti = L // tm
    ang = 2.0 * np.pi * np.outer(np.arange(FOUR_GROUP_DIM), np.arange(FOUR_GROUP_DIM)) / FOUR_GROUP_DIM
    cs = jnp.asarray(np.concatenate([np.cos(ang), np.sin(ang)], axis=1), F32).astype(BF16)
    return pl.pallas_call(
        _chan_dft_kernel,
        grid=(B, nti),
        in_specs=[pl.BlockSpec((tm, C), lambda b, i: (b * nti + i, 0)),
                  pl.BlockSpec((FOUR_GROUP_DIM, 2 * FOUR_GROUP_DIM), lambda b, i: (0, 0))],
        out_specs=pl.BlockSpec((None, 2, tm, C), lambda b, i: (b, 0, i, 0)),
        out_shape=jax.ShapeDtypeStruct((B, 2, L, C), BF16),
        compiler_params=_cparams("parallel", "parallel"),
        name="chan_dft",
    )(u, cs)


def _dense_four_kernel(d_ref, z_ref, wf_ref, g_ref, t_ref, o_ref, acc_s, *, scale):
    k = pl.program_id(2)

    @pl.when(k == 0)
    def _():
        acc_s[...] = jnp.zeros_like(acc_s)

    acc_s[...] += jnp.dot(d_ref[...], z_ref[...], preferred_element_type=F32)

    @pl.when(k == pl.num_programs(2) - 1)
    def _():
        f = (acc_s[...] * scale).astype(BF16)
        yf = jnp.dot(f, wf_ref[...], preferred_element_type=F32)
        o_ref[...] = (t_ref[...].astype(F32) + _sigmoid(g_ref[...].astype(F32)) * yf).astype(o_ref.dtype)


def _dense_four_out(z4, w_four, proj, gate_off, t1):
    B, _, L, C = z4.shape
    D = w_four.shape[1]
    tm = _tile(L, 512)
    tk = _tile(2 * L, 2048, LANES)
    nti = L // tm
    assert gate_off % D == 0
    goff = gate_off // D
    prod = (jnp.arange(L, dtype=jnp.int32)[:, None] * jnp.arange(L, dtype=jnp.int32)[None, :]) % L
    ang = prod.astype(F32) * (2.0 * math.pi / L)
    dmat = jnp.concatenate([jnp.cos(ang), -jnp.sin(ang)], axis=1).astype(BF16)
    scale = 1.0 / math.sqrt(L * FOUR_GROUP_DIM)
    return pl.pallas_call(
        functools.partial(_dense_four_kernel, scale=scale),
        grid=(B, nti, (2 * L) // tk),
        in_specs=[pl.BlockSpec((tm, tk), lambda b, i, k: (i, k)),
                  pl.BlockSpec((None, tk, C), lambda b, i, k: (b, k, 0)),
                  pl.BlockSpec((C, D), lambda b, i, k: (0, 0)),
                  pl.BlockSpec((tm, D), lambda b, i, k: (b * nti + i, goff)),
                  pl.BlockSpec((tm, D), lambda b, i, k: (b * nti + i, 0))],
        out_specs=pl.BlockSpec((tm, D), lambda b, i, k: (b * nti + i, 0)),
        out_shape=jax.ShapeDtypeStruct((B * L, D), BF16),
        scratch_shapes=[pltpu.VMEM((tm, C), F32)],
        compiler_params=_cparams("parallel", "parallel", "arbitrary"),
        name="dense_four_out",
    )(dmat, z4.reshape(B, 2 * L, C), w_four, proj, t1)


def _ffn_kernel(te_ref, act_ref, x_ref, w1_ref, w3_ref, w2_ref, o_ref, acc_s):
    i = pl.program_id(0)
    j = pl.program_id(1)
    nj = pl.num_programs(1)

    @pl.when(j == 0)
    def _():
        acc_s[...] = jnp.zeros_like(acc_s)

    @pl.when(act_ref[i] > 0)
    def _():
        x = x_ref[...]
        h1 = jnp.dot(x, w1_ref[...], preferred_element_type=F32)
        h3 = jnp.dot(x, w3_ref[...], preferred_element_type=F32)
        h = (_silu(h1) * h3).astype(BF16)
        acc_s[...] += jnp.dot(h, w2_ref[...], preferred_element_type=F32)

    @pl.when(j == nj - 1)
    def _():
        o_ref[...] = acc_s[...].astype(o_ref.dtype)


def _ffn(x, w1, w3, w2, tile_expert, tile_active, tm):
    R, D = x.shape
    E, _, F = w1.shape
    tf = _tile(F, 512, LANES)
    nf = F // tf

    def wcol(i, j, te, act):
        return (te[i], 0, jnp.where(act[i] > 0, j, nf - 1))

    def wrow(i, j, te, act):
        return (te[i], jnp.where(act[i] > 0, j, nf - 1), 0)

    grid_spec = pltpu.PrefetchScalarGridSpec(
        num_scalar_prefetch=2,
        grid=(R // tm, nf),
        in_specs=[pl.BlockSpec((tm, D), lambda i, j, te, act: (i, 0)),
                  pl.BlockSpec((None, D, tf), wcol),
                  pl.BlockSpec((None, D, tf), wcol),
                  pl.BlockSpec((None, tf, D), wrow)],
        out_specs=pl.BlockSpec((tm, D), lambda i, j, te, act: (i, 0)),
        scratch_shapes=[pltpu.VMEM((tm, D), F32)],
    )
    return pl.pallas_call(
        _ffn_kernel,
        grid_spec=grid_spec,
        out_shape=jax.ShapeDtypeStruct((R, D), BF16),
        compiler_params=_cparams("parallel", "arbitrary"),
        name="ffn",
    )(tile_expert, tile_active, x, w1, w3, w2)


def _router_kernel(x_ref, r_ref, w_ref, i_ref, *, E):
    xh, xm, _ = _split3(x_ref[...])
    rh, rm, _ = _split3(r_ref[...])
    logits = (jnp.dot(xh, rh, preferred_element_type=F32) + jnp.dot(xm, rh, preferred_element_type=F32)
              + jnp.dot(xh, rm, preferred_element_type=F32))
    lane = lax.broadcasted_iota(jnp.int32, logits.shape, 1).astype(F32)
    logits = jnp.where(lane < E, logits, -jnp.inf)
    ex = jnp.exp(logits - jnp.max(logits, axis=-1, keepdims=True))
    probs = ex / jnp.sum(ex, axis=-1, keepdims=True)
    m1 = jnp.max(probs, axis=-1, keepdims=True)
    i1 = jnp.min(jnp.where(probs == m1, lane, float(LANES)), axis=-1, keepdims=True)
    rest = jnp.where(lane == i1, -1.0, probs)
    m2 = jnp.max(rest, axis=-1, keepdims=True)
    i2 = jnp.min(jnp.where(rest == m2, lane, float(LANES)), axis=-1, keepdims=True)
    den = m1 + m2
    w_ref[...] = jnp.where(lane == 0.0, m1 / den, jnp.where(lane == 1.0, m2 / den, 0.0))
    i_ref[...] = jnp.where(lane == 0.0, i1, jnp.where(lane == 1.0, i2, 0.0)).astype(jnp.int32)


def _router(x, router):
    T, D = x.shape
    E = router.shape[1]
    tm = _tile(T, 512)
    rp = jnp.zeros((D, LANES), F32).at[:, :E].set(router)
    row = pl.BlockSpec((tm, LANES), lambda i: (i, 0))
    return pl.pallas_call(
        functools.partial(_router_kernel, E=E),
        grid=(T // tm,),
        in_specs=[pl.BlockSpec((tm, D), lambda i: (i, 0)), pl.BlockSpec((D, LANES), lambda i: (0, 0))],
        out_specs=[row, row],
        out_shape=[jax.ShapeDtypeStruct((T, LANES), F32), jax.ShapeDtypeStruct((T, LANES), jnp.int32)],
        compiler_params=_cparams("parallel"),
        name="router",
    )(x, rp)


def _moe(xf, xb, router, we1, we3, we2, ln_g, ln_b):
    T, D = xf.shape
    E = we1.shape[0]
    tm = _tile(T, 1024)
    wt, idx = _router(xf, router)
    flat_e = idx[:, :TOP_K].reshape(-1)
    onehot = (flat_e[:, None] == jnp.arange(E, dtype=jnp.int32)[None, :]).astype(jnp.int32)
    rank = jnp.sum((jnp.cumsum(onehot, axis=0) - onehot) * onehot, axis=1)
    counts = jnp.sum(onehot, axis=0)
    padded = ((counts + tm - 1) // tm) * tm
    ends = jnp.cumsum(padded)
    pos = (ends - padded)[flat_e] + rank
    n_tiles = (TOP_K * T) // tm + E
    src = jnp.zeros((n_tiles * tm,), jnp.int32).at[pos].set(
        jnp.arange(TOP_K * T, dtype=jnp.int32) // TOP_K, unique_indices=True, mode="promise_in_bounds")
    starts = jnp.arange(n_tiles, dtype=jnp.int32) * tm
    tile_expert = jnp.minimum(jnp.searchsorted(ends, starts, side="right"), E - 1).astype(jnp.int32)
    tile_active = (starts < ends[-1]).astype(jnp.int32)
    x_sorted = xb.at[src].get(mode="promise_in_bounds")
    y_sorted = _ffn(x_sorted, we1, we3, we2, tile_expert, tile_active, tm)
    pos2 = pos.reshape(T, TOP_K)
    ya = y_sorted.at[pos2[:, 0]].get(mode="promise_in_bounds", unique_indices=True)
    yb = y_sorted.at[pos2[:, 1]].get(mode="promise_in_bounds", unique_indices=True)
    return _combine_ln(xf, ya, yb, wt, ln_g, ln_b)


def _prep_mixer(w_in, conv_w, conv_b, dt_bias_f, dt_bias_b, a_log_f, a_log_b, d_skip, gnorm_w,
                w_ssd_up, w_four, w_o):
    D = w_in.shape[0]
    DI = w_ssd_up.shape[0]
    CD = conv_b.shape[0]
    H = dt_bias_f.shape[0]
    DF = w_four.shape[0]
    G = SSD_GROUPS
    hpg = H // G
    o_dtf = DI + CD
    o_dtb = o_dtf + H
    o_u = o_dtb + H
    o_g = o_u + DF
    w_main = jnp.concatenate([w_in[:, :o_dtf], w_in[:, o_g:]], axis=1).astype(BF16)
    w_u = w_in[:, o_u:o_g].astype(BF16)

    def dt_rows(f, b, fill=0.0):
        lead = f.shape[:-1]
        out = jnp.full(lead + (G, DT_ROWS), fill, F32)
        out = out.at[..., :hpg].set(f.reshape(lead + (G, hpg)))
        return out.at[..., BWD_ROW0:BWD_ROW0 + hpg].set(b.reshape(lead + (G, hpg)))

    w_dt = dt_rows(w_in[:, o_dtf:o_dtb], w_in[:, o_dtb:o_u]).reshape(D, G * DT_ROWS).astype(BF16)
    dt_bias = dt_rows(dt_bias_f, dt_bias_b)
    a_log = dt_rows(a_log_f, a_log_b)
    a_log_l = jnp.zeros((G, 1, LANES), F32).at[:, 0, :DT_ROWS].set(a_log)
    return dict(
        w_main=w_main, w_u=w_u, w_dt=w_dt,
        conv_w=conv_w.reshape(CONV_K, CD), conv_b=conv_b.reshape(1, CD),
        dt_bias_c=dt_bias[..., None], a_log_c=a_log[..., None], a_log_l=a_log_l,
        d_skip_x=jnp.repeat(d_skip, DI // H).reshape(1, DI), gnorm_w=gnorm_w.reshape(1, DI),
        w_ssd_up=w_ssd_up.astype(BF16), w_four=w_four.astype(BF16), w_o=w_o.astype(BF16),
        DI=DI, P=DI // H, gate_off=o_dtf, D=D)


def _token_mixer(xf, xb, B, L, mp, ln_g, ln_b):
    T = B * L
    G = SSD_GROUPS
    DI, D = mp["DI"], mp["D"]
    proj = _mm(xb, mp["w_main"], BF16)
    u = _mm(xb, mp["w_u"], BF16)
    dt = _mm(xb, mp["w_dt"], F32)
    dt4 = dt.reshape(B, L, G, DT_ROWS).transpose(0, 2, 3, 1)
    v = _ssd(proj.reshape(B, L, -1), dt4, mp["conv_w"], mp["conv_b"], mp["dt_bias_c"], mp["a_log_c"],
             mp["a_log_l"], mp["d_skip_x"], mp["gnorm_w"], DI, mp["P"])
    t1 = _gate_mm(v.reshape(T, DI), mp["w_ssd_up"], proj, mp["gate_off"])
    if L <= DENSE_DFT_MAX_L:
        merged = _dense_four_out(_chan_dft(u, B, L), mp["w_four"], proj, mp["gate_off"] + D, t1)
    else:
        g5 = _fft_stage1(u.reshape(B, L, -1), L // FFT_L2)
        y4 = _fft_stage2(g5)
        merged = _four_out(y4, mp["w_four"], proj, mp["gate_off"] + D, t1)
    return _mm_res_ln(merged, mp["w_o"], xf, ln_g, ln_b)


def kernel(x_prompt, x_sample, ln_in_g, ln_in_b, w_in_0, conv_w_0, conv_b_0, dt_bias_f_0, dt_bias_b_0, a_log_f_0, a_log_b_0, d_skip_0, gnorm_w_0, w_ssd_up_0, w_four_0, w_o_0, ln1_g_0, ln1_b_0, w1_0, w3_0, w2_0, ln2_g_0, ln2_b_0, w_in_1, conv_w_1, conv_b_1, dt_bias_f_1, dt_bias_b_1, a_log_f_1, a_log_b_1, d_skip_1, gnorm_w_1, w_ssd_up_1, w_four_1, w_o_1, ln1_g_1, ln1_b_1, router_1, we1_1, we3_1, we2_1, ln2_g_1, ln2_b_1):
    mp0 = _prep_mixer(w_in_0, conv_w_0, conv_b_0, dt_bias_f_0, dt_bias_b_0, a_log_f_0, a_log_b_0,
                      d_skip_0, gnorm_w_0, w_ssd_up_0, w_four_0, w_o_0)
    mp1 = _prep_mixer(w_in_1, conv_w_1, conv_b_1, dt_bias_f_1, dt_bias_b_1, a_log_f_1, a_log_b_1,
                      d_skip_1, gnorm_w_1, w_ssd_up_1, w_four_1, w_o_1)
    w1 = w1_0.astype(BF16)[None]
    w3 = w3_0.astype(BF16)[None]
    w2 = w2_0.astype(BF16)[None]
    we1 = we1_1.astype(BF16)
    we3 = we3_1.astype(BF16)
    we2 = we2_1.astype(BF16)

    def trunk(x):
        B, L, D = x.shape
        T = B * L
        xf, xb = _ln_in(x.reshape(T, D), ln_in_g, ln_in_b)
        xf, xb = _token_mixer(xf, xb, B, L, mp0, ln1_g_0, ln1_b_0)
        tm = _tile(T, 1024)
        n_tiles = T // tm
        ffn = _ffn(xb, w1, w3, w2, jnp.zeros((n_tiles,), jnp.int32), jnp.ones((n_tiles,), jnp.int32), tm)
        xf, xb = _res_ln(xf, ffn, ln2_g_0, ln2_b_0)
        xf, xb = _token_mixer(xf, xb, B, L, mp1, ln1_g_1, ln1_b_1)
        xf, _ = _moe(xf, xb, router_1, we1, we3, we2, ln2_g_1, ln2_b_1)
        return xf.reshape(B, L, D)

    return (trunk(x_prompt), trunk(x_sample))
```

```python
import functools
import math

import numpy as np
import jax
import jax.numpy as jnp
from jax import lax
from jax.experimental import pallas as pl
from jax.experimental.pallas import tpu as pltpu

F32 = jnp.float32
BF16 = jnp.bfloat16

SSD_GROUPS = 8
D_STATE = 128
CONV_K = 5
CHUNK = 128
FOUR_GROUPS = 8
FOUR_GROUP_DIM = 128
TOP_K = 2
DEPTH = 2
ALPHA = (2 * DEPTH) ** 0.25
LN_EPS = 1e-5
RMS_EPS = 1e-5

LANES = 128
BF16_SUBLANES = 16
VMEM_LIMIT_BYTES = 56 * 1024 * 1024

DT_ROWS = 16
BWD_ROW0 = 8
FFT_L2 = 128
DENSE_DFT_MAX_L = 2048


def _cparams(*sem):
    return pltpu.CompilerParams(dimension_semantics=sem, vmem_limit_bytes=VMEM_LIMIT_BYTES)


def _tile(n, pref, align=8, divides=()):
    t = min(pref, n)
    t -= t % align
    while t > align:
        if n % t == 0 and all(d % t == 0 for d in divides):
            return t
        t -= align
    return align


def _sigmoid(x):
    return 1.0 / (1.0 + jnp.exp(-x))


def _silu(x):
    return x / (1.0 + jnp.exp(-x))


def _ln_rows(v, g, b):
    mu = jnp.mean(v, axis=-1, keepdims=True)
    d = v - mu
    var = jnp.mean(d * d, axis=-1, keepdims=True)
    return d * lax.rsqrt(var + LN_EPS) * g + b


def _split3(x):
    hi = x.astype(BF16)
    r1 = x - hi.astype(F32)
    mid = r1.astype(BF16)
    lo = (r1 - mid.astype(F32)).astype(BF16)
    return hi, mid, lo


def _ln_in_kernel(x_ref, g_ref, b_ref, xf_ref, xb_ref):
    y = _ln_rows(x_ref[...], g_ref[...], b_ref[...])
    xf_ref[...] = y
    xb_ref[...] = y.astype(BF16)


def _ln_in(x, g, b):
    T, D = x.shape
    tm = _tile(T, 512)
    row = pl.BlockSpec((tm, D), lambda i: (i, 0))
    vec = pl.BlockSpec((1, D), lambda i: (0, 0))
    return pl.pallas_call(
        _ln_in_kernel,
        grid=(T // tm,),
        in_specs=[row, vec, vec],
        out_specs=[row, row],
        out_shape=[jax.ShapeDtypeStruct((T, D), F32), jax.ShapeDtypeStruct((T, D), BF16)],
        compiler_params=_cparams("parallel"),
        name="ln_in",
    )(x, g.reshape(1, D), b.reshape(1, D))


def _res_ln_kernel(x_ref, y_ref, g_ref, b_ref, xf_ref, xb_ref):
    v = ALPHA * x_ref[...] + y_ref[...].astype(F32)
    y = _ln_rows(v, g_ref[...], b_ref[...])
    xf_ref[...] = y
    xb_ref[...] = y.astype(BF16)


def _res_ln(x, y, g, b):
    T, D = x.shape
    tm = _tile(T, 512)
    row = pl.BlockSpec((tm, D), lambda i: (i, 0))
    vec = pl.BlockSpec((1, D), lambda i: (0, 0))
    return pl.pallas_call(
        _res_ln_kernel,
        grid=(T // tm,),
        in_specs=[row, row, vec, vec],
        out_specs=[row, row],
        out_shape=[jax.ShapeDtypeStruct((T, D), F32), jax.ShapeDtypeStruct((T, D), BF16)],
        compiler_params=_cparams("parallel"),
        name="res_ln",
    )(x, y, g.reshape(1, D), b.reshape(1, D))


def _combine_ln_kernel(x_ref, ya_ref, yb_ref, w_ref, g_ref, b_ref, xf_ref, xb_ref):
    w = w_ref[...]
    mix = w[:, 0:1] * ya_ref[...].astype(F32) + w[:, 1:2] * yb_ref[...].astype(F32)
    y = _ln_rows(ALPHA * x_ref[...] + mix, g_ref[...], b_ref[...])
    xf_ref[...] = y
    xb_ref[...] = y.astype(BF16)


def _combine_ln(x, ya, yb, w, g, b):
    T, D = x.shape
    tm = _tile(T, 512)
    row = pl.BlockSpec((tm, D), lambda i: (i, 0))
    wsp = pl.BlockSpec((tm, LANES), lambda i: (i, 0))
    vec = pl.BlockSpec((1, D), lambda i: (0, 0))
    return pl.pallas_call(
        _combine_ln_kernel,
        grid=(T // tm,),
        in_specs=[row, row, row, wsp, vec, vec],
        out_specs=[row, row],
        out_shape=[jax.ShapeDtypeStruct((T, D), F32), jax.ShapeDtypeStruct((T, D), BF16)],
        compiler_params=_cparams("parallel"),
        name="combine_ln",
    )(x, ya, yb, w, g.reshape(1, D), b.reshape(1, D))


def _mm_kernel(a_ref, w_ref, o_ref):
    o_ref[...] = jnp.dot(a_ref[...], w_ref[...], preferred_element_type=F32).astype(o_ref.dtype)


def _mm(a, w, out_dtype, tm_pref=1024, tn_pref=1024):
    M, K = a.shape
    N = w.shape[1]
    tm = _tile(M, tm_pref)
    tn = _tile(N, tn_pref, LANES)
    return pl.pallas_call(
        _mm_kernel,
        grid=(M // tm, N // tn),
        in_specs=[pl.BlockSpec((tm, K), lambda i, j: (i, 0)), pl.BlockSpec((K, tn), lambda i, j: (0, j))],
        out_specs=pl.BlockSpec((tm, tn), lambda i, j: (i, j)),
        out_shape=jax.ShapeDtypeStruct((M, N), out_dtype),
        compiler_params=_cparams("parallel", "arbitrary"),
        name="mm",
    )(a, w)


def _mm_split_kernel(a_ref, w_ref, o1_ref, o2_ref):
    r = jnp.dot(a_ref[...], w_ref[...], preferred_element_type=F32)
    n1 = o1_ref.shape[1]
    o1_ref[...] = r[:, :n1].astype(o1_ref.dtype)
    o2_ref[...] = r[:, n1:].astype(o2_ref.dtype)


def _mm_split(a, w, n1, dtype1, dtype2):
    M, K = a.shape
    N = w.shape[1]
    tm = _tile(M, 1024)
    return pl.pallas_call(
        _mm_split_kernel,
        grid=(M // tm,),
        in_specs=[pl.BlockSpec((tm, K), lambda i: (i, 0)), pl.BlockSpec((K, N), lambda i: (0, 0))],
        out_specs=[pl.BlockSpec((tm, n1), lambda i: (i, 0)), pl.BlockSpec((tm, N - n1), lambda i: (i, 0))],
        out_shape=[jax.ShapeDtypeStruct((M, n1), dtype1), jax.ShapeDtypeStruct((M, N - n1), dtype2)],
        compiler_params=_cparams("parallel"),
        name="mm_split",
    )(a, w)


def _gate_mm_kernel(a_ref, w_ref, g_ref, o_ref):
    y = jnp.dot(a_ref[...], w_ref[...], preferred_element_type=F32)
    o_ref[...] = (_sigmoid(g_ref[...].astype(F32)) * y).astype(o_ref.dtype)


def _gate_mm(a, w, proj, gate_off):
    M, K = a.shape
    N = w.shape[1]
    tm = _tile(M, 1024)
    tn = _tile(N, 1024, LANES, divides=(gate_off,))
    goff = gate_off // tn
    return pl.pallas_call(
        _gate_mm_kernel,
        grid=(M // tm, N // tn),
        in_specs=[pl.BlockSpec((tm, K), lambda i, j: (i, 0)),
                  pl.BlockSpec((K, tn), lambda i, j: (0, j)),
                  pl.BlockSpec((tm, tn), lambda i, j: (i, goff + j))],
        out_specs=pl.BlockSpec((tm, tn), lambda i, j: (i, j)),
        out_shape=jax.ShapeDtypeStruct((M, N), BF16),
        compiler_params=_cparams("parallel", "arbitrary"),
        name="gate_mm",
    )(a, w, proj)


def _mm_res_ln_kernel(a_ref, w_ref, x_ref, g_ref, b_ref, xf_ref, xb_ref):
    y = jnp.dot(a_ref[...], w_ref[...], preferred_element_type=F32)
    out = _ln_rows(ALPHA * x_ref[...] + y, g_ref[...], b_ref[...])
    xf_ref[...] = out
    xb_ref[...] = out.astype(BF16)


def _mm_res_ln(a, w, x, g, b):
    M, K = a.shape
    D = w.shape[1]
    tm = _tile(M, 512)
    row = pl.BlockSpec((tm, D), lambda i: (i, 0))
    vec = pl.BlockSpec((1, D), lambda i: (0, 0))
    return pl.pallas_call(
        _mm_res_ln_kernel,
        grid=(M // tm,),
        in_specs=[pl.BlockSpec((tm, K), lambda i: (i, 0)), pl.BlockSpec((K, D), lambda i: (0, 0)), row, vec, vec],
        out_specs=[row, row],
        out_shape=[jax.ShapeDtypeStruct((M, D), F32), jax.ShapeDtypeStruct((M, D), BF16)],
        compiler_params=_cparams("parallel"),
        name="mm_res_ln",
    )(a, w, x, g.reshape(1, D), b.reshape(1, D))


PACK_DT = 0
PACK_STATE = DT_ROWS
PACK_OFF = 2 * DT_ROWS
PACK_E = 3 * DT_ROWS


def _conv_silu_chunk(src_ref, w, bias, c, nc, L):
    halo = BF16_SUBLANES
    n_ext = CHUNK + 2 * halo
    W = src_ref.shape[1]
    s0 = pl.multiple_of(c * CHUNK, CHUNK)
    sp = pl.multiple_of(jnp.maximum(s0 - halo, 0), halo)
    sn = pl.multiple_of(jnp.minimum(s0 + CHUNK, L - halo), halo)
    cur = src_ref[pl.ds(s0, CHUNK), :].astype(F32)
    prev = jnp.where(c > 0, src_ref[pl.ds(sp, halo), :].astype(F32), 0.0)
    nxt = jnp.where(c < nc - 1, src_ref[pl.ds(sn, halo), :].astype(F32), 0.0)
    ext = jnp.concatenate([prev, cur, nxt], axis=0)
    acc = jnp.broadcast_to(bias, (CHUNK, W))
    for k in range(CONV_K):
        shift = (n_ext - halo - (k - CONV_K // 2)) % n_ext
        acc = acc + w[k:k + 1, :] * pltpu.roll(ext, shift, 0)[:CHUNK]
    return _silu(acc)


def _ssd_kernel(z_ref, xs_ref, b_ref, c_ref, dt_ref, cwx_ref, cwb_ref, cwc_ref, cbx_ref, cbb_ref, cbc_ref,
                dtb_ref, alc_ref, all_ref, dsk_ref, gnw_ref, o_ref,
                xs_s, c_s, bt_s, cb_s, pack_s, erow_s, yb_s, stf_s, stb_s, *, L, P):
    nc = L // CHUNK
    half = nc // 2
    gw = xs_s.shape[1]

    ri = lax.broadcasted_iota(jnp.int32, (CHUNK, CHUNK), 0)
    ci = lax.broadcasted_iota(jnp.int32, (CHUNK, CHUNK), 1)
    incl_upper = (ri <= ci).astype(BF16)
    row16 = lax.broadcasted_iota(jnp.int32, (DT_ROWS, CHUNK), 0)
    zpad = jnp.zeros((CHUNK - DT_ROWS, CHUNK), F32)
    a_c = -jnp.exp(alc_ref[...])
    a_l = -jnp.exp(all_ref[...])
    dt_bias = dtb_ref[...]
    wx, wb, wc = cwx_ref[...], cwb_ref[...], cwc_ref[...]
    bx, bb, bc_bias = cbx_ref[...], cbb_ref[...], cbc_ref[...]

    def prepass(c, carry):
        s0 = pl.multiple_of(c * CHUNK, CHUNK)
        xs_s[pl.ds(s0, CHUNK), :] = _conv_silu_chunk(xs_ref, wx, bx, c, nc, L).astype(BF16)
        bcv = _conv_silu_chunk(b_ref, wb, bb, c, nc, L).astype(BF16)
        ccv = _conv_silu_chunk(c_ref, wc, bc_bias, c, nc, L).astype(BF16)
        c_s[pl.ds(s0, CHUNK), :] = ccv
        bt_s[:, pl.ds(s0, CHUNK)] = bcv.astype(F32).T.astype(BF16)
        cb = lax.dot_general(ccv, bcv, (((1,), (1,)), ((), ())), preferred_element_type=F32)
        cb_s[pl.ds(s0, CHUNK), :] = cb.astype(BF16)
        raw = dt_ref[:, pl.ds(s0, CHUNK)] + dt_bias
        dtv = jnp.maximum(raw, 0.0) + jnp.log1p(jnp.exp(-jnp.abs(raw)))
        a_row = dtv * a_c
        hi, mid, lo = _split3(a_row)
        acum_row = (jnp.dot(hi, incl_upper, preferred_element_type=F32)
                    + jnp.dot(mid, incl_upper, preferred_element_type=F32)
                    + jnp.dot(lo, incl_upper, preferred_element_type=F32))
        erow_s[:, pl.ds(s0, CHUNK)] = jnp.where(row16 < BWD_ROW0, acum_row, acum_row - a_row)
        acum_col = jnp.concatenate([acum_row, zpad], axis=0).T
        dt_col = jnp.concatenate([dtv, zpad], axis=0).T
        total = acum_col[CHUNK - 1:CHUNK, :]
        fwd_lane = ci < BWD_ROW0
        e_bwd = acum_col - dt_col * a_l
        e_col = jnp.where(fwd_lane, acum_col, e_bwd)
        f_state = dt_col * jnp.exp(jnp.where(fwd_lane, total - acum_col, e_bwd))
        f_off = jnp.exp(jnp.where(fwd_lane, acum_col, total - e_bwd))
        pack = jnp.where(ci < PACK_STATE, dt_col,
                         jnp.where(ci < PACK_OFF, pltpu.roll(f_state, PACK_STATE, 1),
                                   jnp.where(ci < PACK_E, pltpu.roll(f_off, PACK_OFF, 1),
                                             jnp.where(ci < PACK_E + DT_ROWS, pltpu.roll(e_col, PACK_E, 1), 0.0))))
        pack_s[pl.ds(s0, CHUNK), :] = pack
        return carry

    prepass(0, 0)
    prepass(nc - 1, 0)

    def expand_matrix(row0):
        ej = lax.broadcasted_iota(jnp.int32, (CHUNK, gw), 0)
        ec = lax.broadcasted_iota(jnp.int32, (CHUNK, gw), 1)
        blocks = []
        for f in (PACK_DT, PACK_STATE, PACK_OFF):
            lo = (ej - f - row0) * P
            blocks.append(((ec >= lo) & (ec < lo + P)).astype(BF16))
        return jnp.concatenate(blocks, axis=1)

    expand_f = expand_matrix(0)
    expand_b = expand_matrix(BWD_ROW0)
    mask_f = ci <= ri
    mask_b = ci >= ri
    stf_s[...] = jnp.zeros_like(stf_s)
    stb_s[...] = jnp.zeros_like(stb_s)

    def scan_chunk(c, bwd):
        row0 = BWD_ROW0 if bwd else 0
        st_ref = stb_s if bwd else stf_s
        s0 = pl.multiple_of(c * CHUNK, CHUNK)
        pk = pack_s[pl.ds(s0, CHUNK), :]
        ex = jnp.dot(pk.astype(BF16), expand_b if bwd else expand_f, preferred_element_type=F32)
        dt_x = ex[:, :gw]
        fs_x = ex[:, gw:2 * gw]
        fo_x = ex[:, 2 * gw:]
        xc = xs_s[pl.ds(s0, CHUNK), :].astype(F32)
        x_dt = (xc * dt_x).astype(BF16)
        x_st = (xc * fs_x).astype(BF16)
        cc = c_s[pl.ds(s0, CHUNK), :]
        cbm = jnp.where(mask_b if bwd else mask_f, cb_s[pl.ds(s0, CHUNK), :].astype(F32), 0.0)
        state = st_ref[...]
        y = jnp.dot(cc, state.astype(BF16), preferred_element_type=F32) * fo_x
        er = erow_s[:, pl.ds(s0, CHUNK)]
        per_tile = LANES // P
        parts = []
        for q in range(gw // LANES):
            tile = x_dt[:, q * LANES:(q + 1) * LANES]
            acc = None
            for r in range(per_tile):
                j = row0 + q * per_tile + r
                col = pk[:, PACK_E + j:PACK_E + j + 1]
                row = er[j:j + 1, :]
                seg = (row - col) if bwd else (col - row)
                lm = (jnp.exp(jnp.minimum(seg, 0.0)) * cbm).astype(BF16)
                rhs = jnp.where((ci >= r * P) & (ci < (r + 1) * P), tile, jnp.zeros_like(tile))
                d = jnp.dot(lm, rhs, preferred_element_type=F32)
                acc = d if acc is None else acc + d
            parts.append(acc)
        y = y + jnp.concatenate(parts, axis=1)
        e_total = fo_x[0:1, :] if bwd else fo_x[CHUNK - 1:CHUNK, :]
        st_ref[...] = state * e_total + jnp.dot(bt_s[:, pl.ds(s0, CHUNK)], x_st, preferred_element_type=F32)
        return y, xc

    def finish(y, xc, c):
        s0 = pl.multiple_of(c * CHUNK, CHUNK)
        v = (y + xc * dsk_ref[...]) * _silu(z_ref[pl.ds(s0, CHUNK), :].astype(F32))
        ms = jnp.mean(v * v, axis=-1, keepdims=True)
        o_ref[pl.ds(s0, CHUNK), :] = (v * lax.rsqrt(ms + RMS_EPS) * gnw_ref[...]).astype(o_ref.dtype)

    def first_half(i, carry, prepare_next=True):
        cf = i
        cb_ = nc - 1 - i
        yf, _ = scan_chunk(cf, False)
        yb, _ = scan_chunk(cb_, True)
        o_ref[pl.ds(pl.multiple_of(cf * CHUNK, CHUNK), CHUNK), :] = yf.astype(o_ref.dtype)
        yb_s[pl.ds(pl.multiple_of((cb_ - half) * CHUNK, CHUNK), CHUNK), :] = yb.astype(yb_s.dtype)
        if prepare_next:
            prepass(cf + 1, 0)
            prepass(cb_ - 1, 0)
        return carry

    def second_half(i, carry):
        cf = i
        cb_ = nc - 1 - i
        yf, xf = scan_chunk(cf, False)
        yb, xb = scan_chunk(cb_, True)
        yb_prev = yb_s[pl.ds(pl.multiple_of((cf - half) * CHUNK, CHUNK), CHUNK), :].astype(F32)
        yf_prev = o_ref[pl.ds(pl.multiple_of(cb_ * CHUNK, CHUNK), CHUNK), :].astype(F32)
        finish(yf + yb_prev, xf, cf)
        finish(yb + yf_prev, xb, cb_)
        return carry

    lax.fori_loop(0, half - 1, first_half, 0)
    first_half(half - 1, 0, prepare_next=False)
    lax.fori_loop(half, nc, second_half, 0)


def _ssd(proj3, dt4, conv_w, conv_b, dt_bias_c, a_log_c, a_log_l, d_skip_x, gnorm_w, DI, P):
    B, L, _ = proj3.shape
    G = SSD_GROUPS
    gw = DI // G
    assert D_STATE == LANES and L % (2 * CHUNK) == 0 and gw % LANES == 0 and LANES % P == 0 and gw // P <= BWD_ROW0
    xoff = DI // gw
    boff = 2 * DI // LANES
    coff = boff + G
    cwb_off = DI // LANES
    io_bytes = L * (3 * gw + 2 * D_STATE) * 2
    scratch_bytes = L * (gw + 3 * D_STATE) * 2 + L * LANES * 4 + (L // 2) * gw * 2
    single = 2 * io_bytes + scratch_bytes > (VMEM_LIMIT_BYTES * 3) // 4

    def big(shape, imap):
        if single:
            return pl.BlockSpec(shape, imap, pipeline_mode=pl.Buffered(1))
        return pl.BlockSpec(shape, imap)

    in_specs = [
        big((None, L, gw), lambda b, g: (b, 0, g)),
        big((None, L, gw), lambda b, g: (b, 0, xoff + g)),
        big((None, L, D_STATE), lambda b, g: (b, 0, boff + g)),
        big((None, L, D_STATE), lambda b, g: (b, 0, coff + g)),
        pl.BlockSpec((None, None, DT_ROWS, L), lambda b, g: (b, g, 0, 0)),
        pl.BlockSpec((CONV_K, gw), lambda b, g: (0, g)),
        pl.BlockSpec((CONV_K, D_STATE), lambda b, g: (0, cwb_off + g)),
        pl.BlockSpec((CONV_K, D_STATE), lambda b, g: (0, cwb_off + G + g)),
        pl.BlockSpec((1, gw), lambda b, g: (0, g)),
        pl.BlockSpec((1, D_STATE), lambda b, g: (0, cwb_off + g)),
        pl.BlockSpec((1, D_STATE), lambda b, g: (0, cwb_off + G + g)),
        pl.BlockSpec((None, DT_ROWS, 1), lambda b, g: (g, 0, 0)),
        pl.BlockSpec((None, DT_ROWS, 1), lambda b, g: (g, 0, 0)),
        pl.BlockSpec((None, 1, LANES), lambda b, g: (g, 0, 0)),
        pl.BlockSpec((1, gw), lambda b, g: (0, g)),
        pl.BlockSpec((1, gw), lambda b, g: (0, g)),
    ]
    scratch = [
        pltpu.VMEM((L, gw), BF16),
        pltpu.VMEM((L, D_STATE), BF16),
        pltpu.VMEM((D_STATE, L), BF16),
        pltpu.VMEM((L, CHUNK), BF16),
        pltpu.VMEM((L, LANES), F32),
        pltpu.VMEM((DT_ROWS, L), F32),
        pltpu.VMEM((L // 2, gw), BF16),
        pltpu.VMEM((D_STATE, gw), F32),
        pltpu.VMEM((D_STATE, gw), F32),
    ]
    return pl.pallas_call(
        functools.partial(_ssd_kernel, L=L, P=P),
        grid=(B, G),
        in_specs=in_specs,
        out_specs=big((None, L, gw), lambda b, g: (b, 0, g)),
        out_shape=jax.ShapeDtypeStruct((B, L, DI), BF16),
        scratch_shapes=scratch,
        compiler_params=_cparams("parallel", "parallel"),
        name="ssd",
    )(proj3, proj3, proj3, proj3, dt4, conv_w, conv_w, conv_w, conv_b, conv_b, conv_b,
      dt_bias_c, a_log_c, a_log_l, d_skip_x, gnorm_w)


def _fft1_kernel(w_ref, tw_ref, u_ref, o_ref, *, L1, nl2, C):
    res = jnp.dot(w_ref[...], u_ref[...], preferred_element_type=F32)
    rep = C // LANES
    for q in range(nl2):
        a = res[:L1, q * C:(q + 1) * C]
        b = res[L1:, q * C:(q + 1) * C]
        twr = jnp.concatenate([tw_ref[0, q]] * rep, axis=1)
        twi = jnp.concatenate([tw_ref[1, q]] * rep, axis=1)
        o_ref[0, q] = (a * twr - b * twi).astype(o_ref.dtype)
        o_ref[1, q] = (a * twi + b * twr).astype(o_ref.dtype)


def _fft_stage1(u3, L1):
    B, L, C = u3.shape
    L2 = L // L1
    nl2 = 8
    k1 = np.arange(L1)
    ang1 = 2.0 * np.pi * np.outer(k1, np.arange(L1)) / L1
    w1 = jnp.asarray(np.concatenate([np.cos(ang1), -np.sin(ang1)], axis=0), F32).astype(BF16)
    angt = 2.0 * np.pi * np.outer(np.arange(L2), k1) / L
    tw = np.stack([np.cos(angt), -np.sin(angt)], axis=0)[..., None]
    tw = jnp.asarray(np.broadcast_to(tw, (2, L2, L1, LANES)), F32)
    uv = u3.reshape(B, L1, L2 * C)
    return pl.pallas_call(
        functools.partial(_fft1_kernel, L1=L1, nl2=nl2, C=C),
        grid=(B, L2 // nl2),
        in_specs=[pl.BlockSpec((2 * L1, L1), lambda b, j: (0, 0)),
                  pl.BlockSpec((2, nl2, L1, LANES), lambda b, j: (0, j, 0, 0)),
                  pl.BlockSpec((None, L1, nl2 * C), lambda b, j: (b, 0, j))],
        out_specs=pl.BlockSpec((None, 2, nl2, L1, C), lambda b, j: (b, 0, j, 0, 0)),
        out_shape=jax.ShapeDtypeStruct((B, 2, L2, L1, C), BF16),
        compiler_params=_cparams("parallel", "parallel"),
        name="fft_stage1",
    )(w1, tw, uv)


def _fft_stage2(g5):
    B, _, L2, L1, C = g5.shape
    ang = 2.0 * np.pi * np.outer(np.arange(L2), np.arange(L2)) / L2
    cs, sn = np.cos(ang), np.sin(ang)
    w2 = jnp.asarray(np.block([[cs, sn], [-sn, cs]]), F32).astype(BF16)
    N = L1 * C
    tn = _tile(N, 4096, LANES)
    gv = g5.reshape(B, 2 * L2, N)
    y = pl.pallas_call(
        _mm_kernel,
        grid=(B, N // tn),
        in_specs=[pl.BlockSpec((2 * L2, 2 * L2), lambda b, j: (0, 0)),
                  pl.BlockSpec((None, 2 * L2, tn), lambda b, j: (b, 0, j))],
        out_specs=pl.BlockSpec((None, 2 * L2, tn), lambda b, j: (b, 0, j)),
        out_shape=jax.ShapeDtypeStruct((B, 2 * L2, N), BF16),
        compiler_params=_cparams("parallel", "parallel"),
        name="fft_stage2",
    )(w2, gv)
    return y.reshape(B, 2, L2 * L1, C)


def _four_out_kernel(y_ref, cs_ref, wf_ref, g_ref, t_ref, o_ref, f_s, *, scale):
    @pl.when(pl.program_id(2) == 0)
    def _():
        yr = y_ref[0]
        yi = y_ref[1]
        cs = cs_ref[...]
        parts = []
        for g in range(FOUR_GROUPS):
            sl = slice(g * FOUR_GROUP_DIM, (g + 1) * FOUR_GROUP_DIM)
            lhs = jnp.concatenate([yr[:, sl], yi[:, sl]], axis=1)
            parts.append(jnp.dot(lhs, cs, preferred_element_type=F32))
        f_s[...] = (jnp.concatenate(parts, axis=1) * scale).astype(f_s.dtype)

    yf = jnp.dot(f_s[...], wf_ref[...], preferred_element_type=F32)
    o_ref[...] = (t_ref[...].astype(F32) + _sigmoid(g_ref[...].astype(F32)) * yf).astype(o_ref.dtype)


def _four_out(y4, w_four, proj, gate_off, t1):
    B, _, L, C = y4.shape
    D = w_four.shape[1]
    tm = _tile(L, 1024)
    tn = _tile(D, 2048, LANES, divides=(gate_off,))
    goff = gate_off // tn
    nti = L // tm
    ang = 2.0 * np.pi * np.outer(np.arange(FOUR_GROUP_DIM), np.arange(FOUR_GROUP_DIM)) / FOUR_GROUP_DIM
    cs = jnp.asarray(np.concatenate([np.cos(ang), np.sin(ang)], axis=0), F32).astype(BF16)
    scale = 1.0 / math.sqrt(L * FOUR_GROUP_DIM)
    return pl.pallas_call(
        functools.partial(_four_out_kernel, scale=scale),
        grid=(B, nti, D // tn),
        in_specs=[pl.BlockSpec((None, 2, tm, C), lambda b, i, j: (b, 0, i, 0)),
                  pl.BlockSpec((2 * FOUR_GROUP_DIM, FOUR_GROUP_DIM), lambda b, i, j: (0, 0)),
                  pl.BlockSpec((C, tn), lambda b, i, j: (0, j)),
                  pl.BlockSpec((tm, tn), lambda b, i, j: (b * nti + i, goff + j)),
                  pl.BlockSpec((tm, tn), lambda b, i, j: (b * nti + i, j))],
        out_specs=pl.BlockSpec((tm, tn), lambda b, i, j: (b * nti + i, j)),
        out_shape=jax.ShapeDtypeStruct((B * L, D), BF16),
        scratch_shapes=[pltpu.VMEM((tm, C), BF16)],
        compiler_params=_cparams("parallel", "parallel", "arbitrary"),
        name="four_out",
    )(y4, cs, w_four, proj, t1)


def _chan_dft_kernel(u_ref, cs_ref, o_ref):
    u = u_ref[...]
    cs = cs_ref[...]
    cos_parts, sin_parts = [], []
    for g in range(FOUR_GROUPS):
        r = jnp.dot(u[:, g * FOUR_GROUP_DIM:(g + 1) * FOUR_GROUP_DIM], cs, preferred_element_type=F32)
        cos_parts.append(r[:, :FOUR_GROUP_DIM])
        sin_parts.append(r[:, FOUR_GROUP_DIM:])
    o_ref[0] = jnp.concatenate(cos_parts, axis=1).astype(o_ref.dtype)
    o_ref[1] = jnp.concatenate(sin_parts, axis=1).astype(o_ref.dtype)


def _chan_dft(u, B, L):
    T, C = u.shape
    tm = _tile(L, 1024)
    nti = L // tm
    ang = 2.0 * np.pi * np.outer(np.arange(FOUR_GROUP_DIM), np.arange(FOUR_GROUP_DIM)) / FOUR_GROUP_DIM
    cs = jnp.asarray(np.concatenate([np.cos(ang), np.sin(ang)], axis=1), F32).astype(BF16)
    return pl.pallas_call(
        _chan_dft_kernel,
        grid=(B, nti),
        in_specs=[pl.BlockSpec((tm, C), lambda b, i: (b * nti + i, 0)),
                  pl.BlockSpec((FOUR_GROUP_DIM, 2 * FOUR_GROUP_DIM), lambda b, i: (0, 0))],
        out_specs=pl.BlockSpec((None, 2, tm, C), lambda b, i: (b, 0, i, 0)),
        out_shape=jax.ShapeDtypeStruct((B, 2, L, C), BF16),
        compiler_params=_cparams("parallel", "parallel"),
        name="chan_dft",
    )(u, cs)


def _dense_four_kernel(d_ref, z_ref, wf_ref, g_ref, t_ref, o_ref, acc_s, *, scale):
    k = pl.program_id(2)

    @pl.when(k == 0)
    def _():
        acc_s[...] = jnp.zeros_like(acc_s)

    acc_s[...] += jnp.dot(d_ref[...], z_ref[...], preferred_element_type=F32)

    @pl.when(k == pl.num_programs(2) - 1)
    def _():
        f = (acc_s[...] * scale).astype(BF16)
        yf = jnp.dot(f, wf_ref[...], preferred_element_type=F32)
        o_ref[...] = (t_ref[...].astype(F32) + _sigmoid(g_ref[...].astype(F32)) * yf).astype(o_ref.dtype)


def _dense_four_out(z4, w_four, proj, gate_off, t1):
    B, _, L, C = z4.shape
    D = w_four.shape[1]
    tm = _tile(L, 512)
    tk = _tile(2 * L, 2048, LANES)
    nti = L // tm
    assert gate_off % D == 0
    goff = gate_off // D
    prod = (jnp.arange(L, dtype=jnp.int32)[:, None] * jnp.arange(L, dtype=jnp.int32)[None, :]) % L
    ang = prod.astype(F32) * (2.0 * math.pi / L)
    dmat = jnp.concatenate([jnp.cos(ang), -jnp.sin(ang)], axis=1).astype(BF16)
    scale = 1.0 / math.sqrt(L * FOUR_GROUP_DIM)
    return pl.pallas_call(
        functools.partial(_dense_four_kernel, scale=scale),
        grid=(B, nti, (2 * L) // tk),
        in_specs=[pl.BlockSpec((tm, tk), lambda b, i, k: (i, k)),
                  pl.BlockSpec((None, tk, C), lambda b, i, k: (b, k, 0)),
                  pl.BlockSpec((C, D), lambda b, i, k: (0, 0)),
                  pl.BlockSpec((tm, D), lambda b, i, k: (b * nti + i, goff)),
                  pl.BlockSpec((tm, D), lambda b, i, k: (b * nti + i, 0))],
        out_specs=pl.BlockSpec((tm, D), lambda b, i, k: (b * nti + i, 0)),
        out_shape=jax.ShapeDtypeStruct((B * L, D), BF16),
        scratch_shapes=[pltpu.VMEM((tm, C), F32)],
        compiler_params=_cparams("parallel", "parallel", "arbitrary"),
        name="dense_four_out",
    )(dmat, z4.reshape(B, 2 * L, C), w_four, proj, t1)


def _ffn_kernel(te_ref, act_ref, x_ref, w1_ref, w3_ref, w2_ref, o_ref, acc_s):
    i = pl.program_id(0)
    j = pl.program_id(1)
    nj = pl.num_programs(1)
    active = act_ref[i] > 0

    @pl.when(active)
    def _():
        x = x_ref[...]
        h1 = jnp.dot(x, w1_ref[...], preferred_element_type=F32)
        h3 = jnp.dot(x, w3_ref[...], preferred_element_type=F32)
        h = (_silu(h1) * h3).astype(BF16)
        part = jnp.dot(h, w2_ref[...], preferred_element_type=F32)

        @pl.when(j == 0)
        def _():
            acc_s[...] = part

        @pl.when(j > 0)
        def _():
            acc_s[...] += part

        @pl.when(j == nj - 1)
        def _():
            o_ref[...] = acc_s[...].astype(o_ref.dtype)

    @pl.when(jnp.logical_and(jnp.logical_not(active), j == nj - 1))
    def _():
        o_ref[...] = jnp.zeros_like(o_ref)


def _ffn(x, w1, w3, w2, tile_expert, tile_active, tm):
    R, D = x.shape
    E, _, F = w1.shape
    tf = _tile(F, 512, LANES)
    nf = F // tf

    def wcol(i, j, te, act):
        return (te[i], 0, jnp.where(act[i] > 0, j, nf - 1))

    def wrow(i, j, te, act):
        return (te[i], jnp.where(act[i] > 0, j, nf - 1), 0)

    grid_spec = pltpu.PrefetchScalarGridSpec(
        num_scalar_prefetch=2,
        grid=(R // tm, nf),
        in_specs=[pl.BlockSpec((tm, D), lambda i, j, te, act: (i, 0)),
                  pl.BlockSpec((None, D, tf), wcol),
                  pl.BlockSpec((None, D, tf), wcol),
                  pl.BlockSpec((None, tf, D), wrow)],
        out_specs=pl.BlockSpec((tm, D), lambda i, j, te, act: (i, 0)),
        scratch_shapes=[pltpu.VMEM((tm, D), F32)],
    )
    return pl.pallas_call(
        _ffn_kernel,
        grid_spec=grid_spec,
        out_shape=jax.ShapeDtypeStruct((R, D), BF16),
        compiler_params=_cparams("parallel", "arbitrary"),
        name="ffn",
    )(tile_expert, tile_active, x, w1, w3, w2)


def _router_kernel(x_ref, r_ref, w_ref, i_ref, *, E):
    xh, xm, _ = _split3(x_ref[...])
    rh, rm, _ = _split3(r_ref[...])
    logits = (jnp.dot(xh, rh, preferred_element_type=F32) + jnp.dot(xm, rh, preferred_element_type=F32)
              + jnp.dot(xh, rm, preferred_element_type=F32))
    lane = lax.broadcasted_iota(jnp.int32, logits.shape, 1).astype(F32)
    logits = jnp.where(lane < E, logits, -jnp.inf)
    ex = jnp.exp(logits - jnp.max(logits, axis=-1, keepdims=True))
    probs = ex / jnp.sum(ex, axis=-1, keepdims=True)
    m1 = jnp.max(probs, axis=-1, keepdims=True)
    i1 = jnp.min(jnp.where(probs == m1, lane, float(LANES)), axis=-1, keepdims=True)
    rest = jnp.where(lane == i1, -1.0, probs)
    m2 = jnp.max(rest, axis=-1, keepdims=True)
    i2 = jnp.min(jnp.where(rest == m2, lane, float(LANES)), axis=-1, keepdims=True)
    den = m1 + m2
    w_ref[...] = jnp.where(lane == 0.0, m1 / den, jnp.where(lane == 1.0, m2 / den, 0.0))
    i_ref[...] = jnp.where(lane == 0.0, i1, jnp.where(lane == 1.0, i2, 0.0)).astype(jnp.int32)


def _router(x, router):
    T, D = x.shape
    E = router.shape[1]
    tm = _tile(T, 512)
    rp = jnp.zeros((D, LANES), F32).at[:, :E].set(router)
    row = pl.BlockSpec((tm, LANES), lambda i: (i, 0))
    return pl.pallas_call(
        functools.partial(_router_kernel, E=E),
        grid=(T // tm,),
        in_specs=[pl.BlockSpec((tm, D), lambda i: (i, 0)), pl.BlockSpec((D, LANES), lambda i: (0, 0))],
        out_specs=[row, row],
        out_shape=[jax.ShapeDtypeStruct((T, LANES), F32), jax.ShapeDtypeStruct((T, LANES), jnp.int32)],
        compiler_params=_cparams("parallel"),
        name="router",
    )(x, rp)


def _moe(xf, xb, router, we1, we3, we2, ln_g, ln_b):
    T, D = xf.shape
    E = we1.shape[0]
    tm = _tile(T, 1024)
    wt, idx = _router(xf, router)
    flat_e = idx[:, :TOP_K].reshape(-1)
    onehot = (flat_e[:, None] == jnp.arange(E, dtype=jnp.int32)[None, :]).astype(jnp.int32)
    rank = jnp.sum((jnp.cumsum(onehot, axis=0) - onehot) * onehot, axis=1)
    counts = jnp.sum(onehot, axis=0)
    padded = ((counts + tm - 1) // tm) * tm
    ends = jnp.cumsum(padded)
    pos = (ends - padded)[flat_e] + rank
    n_tiles = (TOP_K * T) // tm + E
    src = jnp.zeros((n_tiles * tm,), jnp.int32).at[pos].set(
        jnp.arange(TOP_K * T, dtype=jnp.int32) // TOP_K, unique_indices=True, mode="promise_in_bounds")
    starts = jnp.arange(n_tiles, dtype=jnp.int32) * tm
    tile_expert = jnp.minimum(jnp.searchsorted(ends, starts, side="right"), E - 1).astype(jnp.int32)
    tile_active = (starts < ends[-1]).astype(jnp.int32)
    x_sorted = xb.at[src].get(mode="promise_in_bounds")
    y_sorted = _ffn(x_sorted, we1, we3, we2, tile_expert, tile_active, tm)
    pos2 = pos.reshape(T, TOP_K)
    ya = y_sorted.at[pos2[:, 0]].get(mode="promise_in_bounds", unique_indices=True)
    yb = y_sorted.at[pos2[:, 1]].get(mode="promise_in_bounds", unique_indices=True)
    return _combine_ln(xf, ya, yb, wt, ln_g, ln_b)


def _prep_mixer(w_in, conv_w, conv_b, dt_bias_f, dt_bias_b, a_log_f, a_log_b, d_skip, gnorm_w,
                w_ssd_up, w_four, w_o):
    D = w_in.shape[0]
    DI = w_ssd_up.shape[0]
    CD = conv_b.shape[0]
    H = dt_bias_f.shape[0]
    DF = w_four.shape[0]
    G = SSD_GROUPS
    hpg = H // G
    o_dtf = DI + CD
    o_dtb = o_dtf + H
    o_u = o_dtb + H
    o_g = o_u + DF
    w_main = jnp.concatenate([w_in[:, :o_dtf], w_in[:, o_g:]], axis=1).astype(BF16)
    w_u = w_in[:, o_u:o_g].astype(BF16)

    def dt_rows(f, b, fill=0.0):
        lead = f.shape[:-1]
        out = jnp.full(lead + (G, DT_ROWS), fill, F32)
        out = out.at[..., :hpg].set(f.reshape(lead + (G, hpg)))
        return out.at[..., BWD_ROW0:BWD_ROW0 + hpg].set(b.reshape(lead + (G, hpg)))

    w_dt = dt_rows(w_in[:, o_dtf:o_dtb], w_in[:, o_dtb:o_u]).reshape(D, G * DT_ROWS).astype(BF16)
    dt_bias = dt_rows(dt_bias_f, dt_bias_b)
    a_log = dt_rows(a_log_f, a_log_b)
    a_log_l = jnp.zeros((G, 1, LANES), F32).at[:, 0, :DT_ROWS].set(a_log)
    return dict(
        w_main=w_main, w_udt=jnp.concatenate([w_u, w_dt], axis=1), w_u_cols=DF,
        conv_w=conv_w.reshape(CONV_K, CD), conv_b=conv_b.reshape(1, CD),
        dt_bias_c=dt_bias[..., None], a_log_c=a_log[..., None], a_log_l=a_log_l,
        d_skip_x=jnp.repeat(d_skip, DI // H).reshape(1, DI), gnorm_w=gnorm_w.reshape(1, DI),
        w_ssd_up=w_ssd_up.astype(BF16), w_four=w_four.astype(BF16), w_o=w_o.astype(BF16),
        DI=DI, P=DI // H, gate_off=o_dtf, D=D)


def _token_mixer(xf, xb, B, L, mp, ln_g, ln_b):
    T = B * L
    G = SSD_GROUPS
    DI, D = mp["DI"], mp["D"]
    proj = _mm(xb, mp["w_main"], BF16)
    u, dt = _mm_split(xb, mp["w_udt"], mp["w_u_cols"], BF16, F32)
    dt4 = dt.reshape(B, L, G, DT_ROWS).transpose(0, 2, 3, 1)
    v = _ssd(proj.reshape(B, L, -1), dt4, mp["conv_w"], mp["conv_b"], mp["dt_bias_c"], mp["a_log_c"],
             mp["a_log_l"], mp["d_skip_x"], mp["gnorm_w"], DI, mp["P"])
    t1 = _gate_mm(v.reshape(T, DI), mp["w_ssd_up"], proj, mp["gate_off"])
    if L <= DENSE_DFT_MAX_L:
        merged = _dense_four_out(_chan_dft(u, B, L), mp["w_four"], proj, mp["gate_off"] + D, t1)
    else:
        g5 = _fft_stage1(u.reshape(B, L, -1), L // FFT_L2)
        y4 = _fft_stage2(g5)
        merged = _four_out(y4, mp["w_four"], proj, mp["gate_off"] + D, t1)
    return _mm_res_ln(merged, mp["w_o"], xf, ln_g, ln_b)


def kernel(x_prompt, x_sample, ln_in_g, ln_in_b, w_in_0, conv_w_0, conv_b_0, dt_bias_f_0, dt_bias_b_0, a_log_f_0, a_log_b_0, d_skip_0, gnorm_w_0, w_ssd_up_0, w_four_0, w_o_0, ln1_g_0, ln1_b_0, w1_0, w3_0, w2_0, ln2_g_0, ln2_b_0, w_in_1, conv_w_1, conv_b_1, dt_bias_f_1, dt_bias_b_1, a_log_f_1, a_log_b_1, d_skip_1, gnorm_w_1, w_ssd_up_1, w_four_1, w_o_1, ln1_g_1, ln1_b_1, router_1, we1_1, we3_1, we2_1, ln2_g_1, ln2_b_1):
    mp0 = _prep_mixer(w_in_0, conv_w_0, conv_b_0, dt_bias_f_0, dt_bias_b_0, a_log_f_0, a_log_b_0,
                      d_skip_0, gnorm_w_0, w_ssd_up_0, w_four_0, w_o_0)
    mp1 = _prep_mixer(w_in_1, conv_w_1, conv_b_1, dt_bias_f_1, dt_bias_b_1, a_log_f_1, a_log_b_1,
                      d_skip_1, gnorm_w_1, w_ssd_up_1, w_four_1, w_o_1)
    w1 = w1_0.astype(BF16)[None]
    w3 = w3_0.astype(BF16)[None]
    w2 = w2_0.astype(BF16)[None]
    we1, we3, we2, x_prompt = lax.optimization_barrier(
        (we1_1.astype(BF16), we3_1.astype(BF16), we2_1.astype(BF16), x_prompt))

    def trunk(x):
        B, L, D = x.shape
        T = B * L
        xf, xb = _ln_in(x.reshape(T, D), ln_in_g, ln_in_b)
        xf, xb = _token_mixer(xf, xb, B, L, mp0, ln1_g_0, ln1_b_0)
        tm = _tile(T, 1024)
        n_tiles = T // tm
        ffn = _ffn(xb, w1, w3, w2, jnp.zeros((n_tiles,), jnp.int32), jnp.ones((n_tiles,), jnp.int32), tm)
        xf, xb = _res_ln(xf, ffn, ln2_g_0, ln2_b_0)
        xf, xb = _token_mixer(xf, xb, B, L, mp1, ln1_g_1, ln1_b_1)
        xf, _ = _moe(xf, xb, router_1, we1, we3, we2, ln2_g_1, ln2_b_1)
        return xf.reshape(B, L, D)

    return (trunk(x_prompt), trunk(x_sample))
```

```python
import functools
import math

import numpy as np
import jax
import jax.numpy as jnp
from jax import lax
from jax.experimental import pallas as pl
from jax.experimental.pallas import tpu as pltpu

F32 = jnp.float32
BF16 = jnp.bfloat16

SSD_GROUPS = 8
D_STATE = 128
CONV_K = 5
CHUNK = 128
FOUR_GROUPS = 8
FOUR_GROUP_DIM = 128
TOP_K = 2
DEPTH = 2
ALPHA = (2 * DEPTH) ** 0.25
LN_EPS = 1e-5
RMS_EPS = 1e-5
LOG2E = math.log2(math.e)

LANES = 128
BF16_SUBLANES = 16
VMEM_LIMIT_BYTES = 56 * 1024 * 1024

DT_ROWS = 16
BWD_ROW0 = 8
FFT_L2 = 128
DENSE_DFT_MAX_L = 2048


def _cparams(*sem):
    return pltpu.CompilerParams(dimension_semantics=sem, vmem_limit_bytes=VMEM_LIMIT_BYTES)


def _tile(n, pref, align=8, divides=()):
    t = min(pref, n)
    t -= t % align
    while t > align:
        if n % t == 0 and all(d % t == 0 for d in divides):
            return t
        t -= align
    return align


def _sigmoid(x):
    return 1.0 / (1.0 + jnp.exp(-x))


def _silu(x):
    hx = 0.5 * x
    return hx + hx * jnp.tanh(hx)


def _ln_rows(v, g, b):
    mu = jnp.mean(v, axis=-1, keepdims=True)
    d = v - mu
    var = jnp.mean(d * d, axis=-1, keepdims=True)
    return d * lax.rsqrt(var + LN_EPS) * g + b


def _split3(x):
    hi = x.astype(BF16)
    r1 = x - hi.astype(F32)
    mid = r1.astype(BF16)
    lo = (r1 - mid.astype(F32)).astype(BF16)
    return hi, mid, lo


def _ln_in_kernel(x_ref, g_ref, b_ref, xf_ref, xb_ref):
    y = _ln_rows(x_ref[...], g_ref[...], b_ref[...])
    xf_ref[...] = y
    xb_ref[...] = y.astype(BF16)


def _ln_in(x, g, b):
    T, D = x.shape
    tm = _tile(T, 512)
    row = pl.BlockSpec((tm, D), lambda i: (i, 0))
    vec = pl.BlockSpec((1, D), lambda i: (0, 0))
    return pl.pallas_call(
        _ln_in_kernel,
        grid=(T // tm,),
        in_specs=[row, vec, vec],
        out_specs=[row, row],
        out_shape=[jax.ShapeDtypeStruct((T, D), F32), jax.ShapeDtypeStruct((T, D), BF16)],
        compiler_params=_cparams("parallel"),
        name="ln_in",
    )(x, g.reshape(1, D), b.reshape(1, D))


def _res_ln_kernel(x_ref, y_ref, g_ref, b_ref, xf_ref, xb_ref):
    v = ALPHA * x_ref[...] + y_ref[...].astype(F32)
    y = _ln_rows(v, g_ref[...], b_ref[...])
    xf_ref[...] = y
    xb_ref[...] = y.astype(BF16)


def _res_ln(x, y, g, b):
    T, D = x.shape
    tm = _tile(T, 512)
    row = pl.BlockSpec((tm, D), lambda i: (i, 0))
    vec = pl.BlockSpec((1, D), lambda i: (0, 0))
    return pl.pallas_call(
        _res_ln_kernel,
        grid=(T // tm,),
        in_specs=[row, row, vec, vec],
        out_specs=[row, row],
        out_shape=[jax.ShapeDtypeStruct((T, D), F32), jax.ShapeDtypeStruct((T, D), BF16)],
        compiler_params=_cparams("parallel"),
        name="res_ln",
    )(x, y, g.reshape(1, D), b.reshape(1, D))


def _combine_ln_kernel(x_ref, ya_ref, yb_ref, w_ref, g_ref, b_ref, xf_ref, xb_ref):
    w = w_ref[...]
    mix = w[:, 0:1] * ya_ref[...].astype(F32) + w[:, 1:2] * yb_ref[...].astype(F32)
    y = _ln_rows(ALPHA * x_ref[...] + mix, g_ref[...], b_ref[...])
    xf_ref[...] = y
    xb_ref[...] = y.astype(BF16)


def _combine_ln(x, ya, yb, w, g, b):
    T, D = x.shape
    tm = _tile(T, 512)
    row = pl.BlockSpec((tm, D), lambda i: (i, 0))
    wsp = pl.BlockSpec((tm, LANES), lambda i: (i, 0))
    vec = pl.BlockSpec((1, D), lambda i: (0, 0))
    return pl.pallas_call(
        _combine_ln_kernel,
        grid=(T // tm,),
        in_specs=[row, row, row, wsp, vec, vec],
        out_specs=[row, row],
        out_shape=[jax.ShapeDtypeStruct((T, D), F32), jax.ShapeDtypeStruct((T, D), BF16)],
        compiler_params=_cparams("parallel"),
        name="combine_ln",
    )(x, ya, yb, w, g.reshape(1, D), b.reshape(1, D))


def _mm_kernel(a_ref, w_ref, o_ref):
    o_ref[...] = jnp.dot(a_ref[...], w_ref[...], preferred_element_type=F32).astype(o_ref.dtype)


def _mm(a, w, out_dtype, tm_pref=1024, tn_pref=1024):
    M, K = a.shape
    N = w.shape[1]
    tm = _tile(M, tm_pref)
    tn = _tile(N, tn_pref, LANES)
    return pl.pallas_call(
        _mm_kernel,
        grid=(M // tm, N // tn),
        in_specs=[pl.BlockSpec((tm, K), lambda i, j: (i, 0)), pl.BlockSpec((K, tn), lambda i, j: (0, j))],
        out_specs=pl.BlockSpec((tm, tn), lambda i, j: (i, j)),
        out_shape=jax.ShapeDtypeStruct((M, N), out_dtype),
        compiler_params=_cparams("parallel", "arbitrary"),
        name="mm",
    )(a, w)


def _mm_split_kernel(a_ref, w_ref, o1_ref, o2_ref):
    r = jnp.dot(a_ref[...], w_ref[...], preferred_element_type=F32)
    n1 = o1_ref.shape[1]
    o1_ref[...] = r[:, :n1].astype(o1_ref.dtype)
    o2_ref[...] = r[:, n1:].astype(o2_ref.dtype)


def _mm_split(a, w, n1, dtype1, dtype2):
    M, K = a.shape
    N = w.shape[1]
    tm = _tile(M, 1024)
    return pl.pallas_call(
        _mm_split_kernel,
        grid=(M // tm,),
        in_specs=[pl.BlockSpec((tm, K), lambda i: (i, 0)), pl.BlockSpec((K, N), lambda i: (0, 0))],
        out_specs=[pl.BlockSpec((tm, n1), lambda i: (i, 0)), pl.BlockSpec((tm, N - n1), lambda i: (i, 0))],
        out_shape=[jax.ShapeDtypeStruct((M, n1), dtype1), jax.ShapeDtypeStruct((M, N - n1), dtype2)],
        compiler_params=_cparams("parallel"),
        name="mm_split",
    )(a, w)


def _gate_mm_kernel(a_ref, w_ref, g_ref, o_ref):
    y = jnp.dot(a_ref[...], w_ref[...], preferred_element_type=F32)
    o_ref[...] = (_sigmoid(g_ref[...].astype(F32)) * y).astype(o_ref.dtype)


def _gate_mm(a, w, proj, gate_off):
    M, K = a.shape
    N = w.shape[1]
    tm = _tile(M, 1024)
    tn = _tile(N, 1024, LANES, divides=(gate_off,))
    goff = gate_off // tn
    return pl.pallas_call(
        _gate_mm_kernel,
        grid=(M // tm, N // tn),
        in_specs=[pl.BlockSpec((tm, K), lambda i, j: (i, 0)),
                  pl.BlockSpec((K, tn), lambda i, j: (0, j)),
                  pl.BlockSpec((tm, tn), lambda i, j: (i, goff + j))],
        out_specs=pl.BlockSpec((tm, tn), lambda i, j: (i, j)),
        out_shape=jax.ShapeDtypeStruct((M, N), BF16),
        compiler_params=_cparams("parallel", "arbitrary"),
        name="gate_mm",
    )(a, w, proj)


def _mm_res_ln_kernel(a_ref, w_ref, x_ref, g_ref, b_ref, xf_ref, xb_ref):
    y = jnp.dot(a_ref[...], w_ref[...], preferred_element_type=F32)
    out = _ln_rows(ALPHA * x_ref[...] + y, g_ref[...], b_ref[...])
    xf_ref[...] = out
    xb_ref[...] = out.astype(BF16)


def _mm_res_ln(a, w, x, g, b):
    M, K = a.shape
    D = w.shape[1]
    tm = _tile(M, 512)
    row = pl.BlockSpec((tm, D), lambda i: (i, 0))
    vec = pl.BlockSpec((1, D), lambda i: (0, 0))
    return pl.pallas_call(
        _mm_res_ln_kernel,
        grid=(M // tm,),
        in_specs=[pl.BlockSpec((tm, K), lambda i: (i, 0)), pl.BlockSpec((K, D), lambda i: (0, 0)), row, vec, vec],
        out_specs=[row, row],
        out_shape=[jax.ShapeDtypeStruct((M, D), F32), jax.ShapeDtypeStruct((M, D), BF16)],
        compiler_params=_cparams("parallel"),
        name="mm_res_ln",
    )(a, w, x, g.reshape(1, D), b.reshape(1, D))


PACK_DT = 0
PACK_STATE = DT_ROWS
PACK_OFF = 2 * DT_ROWS
PACK_E = 3 * DT_ROWS


def _conv_silu_chunk(src_ref, w, bias, c, nc, L):
    halo = BF16_SUBLANES
    n_ext = CHUNK + 2 * halo
    W = src_ref.shape[1]
    s0 = pl.multiple_of(c * CHUNK, CHUNK)
    sp = pl.multiple_of(jnp.maximum(s0 - halo, 0), halo)
    sn = pl.multiple_of(jnp.minimum(s0 + CHUNK, L - halo), halo)
    cur = src_ref[pl.ds(s0, CHUNK), :].astype(F32)
    prev = jnp.where(c > 0, src_ref[pl.ds(sp, halo), :].astype(F32), 0.0)
    nxt = jnp.where(c < nc - 1, src_ref[pl.ds(sn, halo), :].astype(F32), 0.0)
    ext = jnp.concatenate([prev, cur, nxt], axis=0)
    acc = jnp.broadcast_to(bias, (CHUNK, W))
    for k in range(CONV_K):
        shift = (n_ext - halo - (k - CONV_K // 2)) % n_ext
        acc = acc + w[k:k + 1, :] * pltpu.roll(ext, shift, 0)[:CHUNK]
    return _silu(acc)


def _ssd_kernel(z_ref, xs_ref, b_ref, c_ref, dt_ref, cwx_ref, cwb_ref, cwc_ref, cbx_ref, cbb_ref, cbc_ref,
                dtb_ref, alc_ref, all_ref, dsk_ref, gnw_ref, o_ref,
                xs_s, c_s, bt_s, cb_s, pack_s, erow_s, yb_s, stf_s, stb_s, *, L, P):
    nc = L // CHUNK
    half = nc // 2
    gw = xs_s.shape[1]

    ri = lax.broadcasted_iota(jnp.int32, (CHUNK, CHUNK), 0)
    ci = lax.broadcasted_iota(jnp.int32, (CHUNK, CHUNK), 1)
    incl_upper = (ri <= ci).astype(BF16)
    row16 = lax.broadcasted_iota(jnp.int32, (DT_ROWS, CHUNK), 0)
    zpad = jnp.zeros((CHUNK - DT_ROWS, CHUNK), F32)
    a_c = -jnp.exp(alc_ref[...])
    a_l = -jnp.exp(all_ref[...])
    dt_bias = dtb_ref[...]
    wx, wb, wc = cwx_ref[...], cwb_ref[...], cwc_ref[...]
    bx, bb, bc_bias = cbx_ref[...], cbb_ref[...], cbc_ref[...]

    def prepass(c, carry):
        s0 = pl.multiple_of(c * CHUNK, CHUNK)
        xs_s[pl.ds(s0, CHUNK), :] = _conv_silu_chunk(xs_ref, wx, bx, c, nc, L).astype(BF16)
        bcv = _conv_silu_chunk(b_ref, wb, bb, c, nc, L).astype(BF16)
        ccv = _conv_silu_chunk(c_ref, wc, bc_bias, c, nc, L).astype(BF16)
        c_s[pl.ds(s0, CHUNK), :] = ccv
        bt_s[:, pl.ds(s0, CHUNK)] = bcv.astype(F32).T.astype(BF16)
        cb = lax.dot_general(ccv, bcv, (((1,), (1,)), ((), ())), preferred_element_type=F32)
        cb_s[pl.ds(s0, CHUNK), :] = cb.astype(BF16)
        raw = dt_ref[:, pl.ds(s0, CHUNK)] + dt_bias
        dtv = jnp.maximum(raw, 0.0) + jnp.log1p(jnp.exp(-jnp.abs(raw)))
        a_row = dtv * a_c
        hi, mid, lo = _split3(a_row)
        acum_row = (jnp.dot(hi, incl_upper, preferred_element_type=F32)
                    + jnp.dot(mid, incl_upper, preferred_element_type=F32)
                    + jnp.dot(lo, incl_upper, preferred_element_type=F32))
        erow_s[:, pl.ds(s0, CHUNK)] = LOG2E * jnp.where(row16 < BWD_ROW0, acum_row, acum_row - a_row)
        acum_col = jnp.concatenate([acum_row, zpad], axis=0).T
        dt_col = jnp.concatenate([dtv, zpad], axis=0).T
        total = acum_col[CHUNK - 1:CHUNK, :]
        fwd_lane = ci < BWD_ROW0
        e_bwd = acum_col - dt_col * a_l
        e_col = jnp.where(fwd_lane, acum_col, e_bwd)
        f_state = dt_col * jnp.exp(jnp.where(fwd_lane, total - acum_col, e_bwd))
        f_off = jnp.exp(jnp.where(fwd_lane, acum_col, total - e_bwd))
        pack = jnp.where(ci < PACK_STATE, dt_col,
                         jnp.where(ci < PACK_OFF, pltpu.roll(f_state, PACK_STATE, 1),
                                   jnp.where(ci < PACK_E, pltpu.roll(f_off, PACK_OFF, 1),
                                             jnp.where(ci < PACK_E + DT_ROWS, pltpu.roll(LOG2E * e_col, PACK_E, 1), 0.0))))
        pack_s[pl.ds(s0, CHUNK), :] = pack
        return carry

    prepass(0, 0)
    prepass(nc - 1, 0)

    def expand_matrix(row0):
        ej = lax.broadcasted_iota(jnp.int32, (CHUNK, gw), 0)
        ec = lax.broadcasted_iota(jnp.int32, (CHUNK, gw), 1)
        blocks = []
        for f in (PACK_DT, PACK_STATE, PACK_OFF):
            lo = (ej - f - row0) * P
            blocks.append(((ec >= lo) & (ec < lo + P)).astype(BF16))
        return jnp.concatenate(blocks, axis=1)

    expand_f = expand_matrix(0)
    expand_b = expand_matrix(BWD_ROW0)
    mask_f = ci <= ri
    mask_b = ci >= ri
    stf_s[...] = jnp.zeros_like(stf_s)
    stb_s[...] = jnp.zeros_like(stb_s)

    def scan_chunk(c, bwd):
        row0 = BWD_ROW0 if bwd else 0
        st_ref = stb_s if bwd else stf_s
        s0 = pl.multiple_of(c * CHUNK, CHUNK)
        pk = pack_s[pl.ds(s0, CHUNK), :]
        ex = jnp.dot(pk.astype(BF16), expand_b if bwd else expand_f, preferred_element_type=F32)
        dt_x = ex[:, :gw]
        fs_x = ex[:, gw:2 * gw]
        fo_x = ex[:, 2 * gw:]
        xc = xs_s[pl.ds(s0, CHUNK), :].astype(F32)
        x_dt = (xc * dt_x).astype(BF16)
        x_st = (xc * fs_x).astype(BF16)
        cc = c_s[pl.ds(s0, CHUNK), :]
        cbm = jnp.where(mask_b if bwd else mask_f, cb_s[pl.ds(s0, CHUNK), :].astype(F32), 0.0)
        state = st_ref[...]
        y = jnp.dot(cc, state.astype(BF16), preferred_element_type=F32) * fo_x
        er = erow_s[:, pl.ds(s0, CHUNK)]
        per_tile = LANES // P
        parts = []
        for q in range(gw // LANES):
            tile = x_dt[:, q * LANES:(q + 1) * LANES]
            acc = None
            for r in range(per_tile):
                j = row0 + q * per_tile + r
                col = pk[:, PACK_E + j:PACK_E + j + 1]
                row = er[j:j + 1, :]
                seg = (row - col) if bwd else (col - row)
                lm = (jnp.exp2(jnp.minimum(seg, 0.0)) * cbm).astype(BF16)
                rhs = jnp.where((ci >= r * P) & (ci < (r + 1) * P), tile, jnp.zeros_like(tile))
                d = jnp.dot(lm, rhs, preferred_element_type=F32)
                acc = d if acc is None else acc + d
            parts.append(acc)
        y = y + jnp.concatenate(parts, axis=1)
        e_total = fo_x[0:1, :] if bwd else fo_x[CHUNK - 1:CHUNK, :]
        st_ref[...] = state * e_total + jnp.dot(bt_s[:, pl.ds(s0, CHUNK)], x_st, preferred_element_type=F32)
        return y, xc

    def finish(y, xc, c):
        s0 = pl.multiple_of(c * CHUNK, CHUNK)
        v = (y + xc * dsk_ref[...]) * _silu(z_ref[pl.ds(s0, CHUNK), :].astype(F32))
        ms = jnp.mean(v * v, axis=-1, keepdims=True)
        o_ref[pl.ds(s0, CHUNK), :] = (v * lax.rsqrt(ms + RMS_EPS) * gnw_ref[...]).astype(o_ref.dtype)

    def first_half(i, carry, prepare_next=True):
        cf = i
        cb_ = nc - 1 - i
        yf, _ = scan_chunk(cf, False)
        yb, _ = scan_chunk(cb_, True)
        o_ref[pl.ds(pl.multiple_of(cf * CHUNK, CHUNK), CHUNK), :] = yf.astype(o_ref.dtype)
        yb_s[pl.ds(pl.multiple_of((cb_ - half) * CHUNK, CHUNK), CHUNK), :] = yb.astype(yb_s.dtype)
        if prepare_next:
            prepass(cf + 1, 0)
            prepass(cb_ - 1, 0)
        return carry

    def second_half(i, carry):
        cf = i
        cb_ = nc - 1 - i
        yf, xf = scan_chunk(cf, False)
        yb, xb = scan_chunk(cb_, True)
        yb_prev = yb_s[pl.ds(pl.multiple_of((cf - half) * CHUNK, CHUNK), CHUNK), :].astype(F32)
        yf_prev = o_ref[pl.ds(pl.multiple_of(cb_ * CHUNK, CHUNK), CHUNK), :].astype(F32)
        finish(yf + yb_prev, xf, cf)
        finish(yb + yf_prev, xb, cb_)
        return carry

    lax.fori_loop(0, half - 1, first_half, 0)
    first_half(half - 1, 0, prepare_next=False)
    lax.fori_loop(half, nc, second_half, 0, unroll=2)


def _ssd(proj3, dt4, conv_w, conv_b, dt_bias_c, a_log_c, a_log_l, d_skip_x, gnorm_w, DI, P):
    B, L, _ = proj3.shape
    G = SSD_GROUPS
    gw = DI // G
    assert D_STATE == LANES and L % (2 * CHUNK) == 0 and gw % LANES == 0 and LANES % P == 0 and gw // P <= BWD_ROW0
    xoff = DI // gw
    boff = 2 * DI // LANES
    coff = boff + G
    cwb_off = DI // LANES
    io_bytes = L * (3 * gw + 2 * D_STATE) * 2
    scratch_bytes = L * (gw + 3 * D_STATE) * 2 + L * LANES * 4 + (L // 2) * gw * 2
    single = 2 * io_bytes + scratch_bytes > (VMEM_LIMIT_BYTES * 3) // 4

    def big(shape, imap):
        if single:
            return pl.BlockSpec(shape, imap, pipeline_mode=pl.Buffered(1))
        return pl.BlockSpec(shape, imap)

    in_specs = [
        big((None, L, gw), lambda b, g: (b, 0, g)),
        big((None, L, gw), lambda b, g: (b, 0, xoff + g)),
        big((None, L, D_STATE), lambda b, g: (b, 0, boff + g)),
        big((None, L, D_STATE), lambda b, g: (b, 0, coff + g)),
        pl.BlockSpec((None, None, DT_ROWS, L), lambda b, g: (b, g, 0, 0)),
        pl.BlockSpec((CONV_K, gw), lambda b, g: (0, g)),
        pl.BlockSpec((CONV_K, D_STATE), lambda b, g: (0, cwb_off + g)),
        pl.BlockSpec((CONV_K, D_STATE), lambda b, g: (0, cwb_off + G + g)),
        pl.BlockSpec((1, gw), lambda b, g: (0, g)),
        pl.BlockSpec((1, D_STATE), lambda b, g: (0, cwb_off + g)),
        pl.BlockSpec((1, D_STATE), lambda b, g: (0, cwb_off + G + g)),
        pl.BlockSpec((None, DT_ROWS, 1), lambda b, g: (g, 0, 0)),
        pl.BlockSpec((None, DT_ROWS, 1), lambda b, g: (g, 0, 0)),
        pl.BlockSpec((None, 1, LANES), lambda b, g: (g, 0, 0)),
        pl.BlockSpec((1, gw), lambda b, g: (0, g)),
        pl.BlockSpec((1, gw), lambda b, g: (0, g)),
    ]
    scratch = [
        pltpu.VMEM((L, gw), BF16),
        pltpu.VMEM((L, D_STATE), BF16),
        pltpu.VMEM((D_STATE, L), BF16),
        pltpu.VMEM((L, CHUNK), BF16),
        pltpu.VMEM((L, LANES), F32),
        pltpu.VMEM((DT_ROWS, L), F32),
        pltpu.VMEM((L // 2, gw), BF16),
        pltpu.VMEM((D_STATE, gw), F32),
        pltpu.VMEM((D_STATE, gw), F32),
    ]
    return pl.pallas_call(
        functools.partial(_ssd_kernel, L=L, P=P),
        grid=(B, G),
        in_specs=in_specs,
        out_specs=big((None, L, gw), lambda b, g: (b, 0, g)),
        out_shape=jax.ShapeDtypeStruct((B, L, DI), BF16),
        scratch_shapes=scratch,
        compiler_params=_cparams("parallel", "parallel"),
        name="ssd",
    )(proj3, proj3, proj3, proj3, dt4, conv_w, conv_w, conv_w, conv_b, conv_b, conv_b,
      dt_bias_c, a_log_c, a_log_l, d_skip_x, gnorm_w)


def _fft1_kernel(w_ref, tw_ref, u_ref, o_ref, *, L1, nl2, C):
    res = jnp.dot(w_ref[...], u_ref[...], preferred_element_type=F32)
    rep = C // LANES
    for q in range(nl2):
        a = res[:L1, q * C:(q + 1) * C]
        b = res[L1:, q * C:(q + 1) * C]
        twr = jnp.concatenate([tw_ref[0, q]] * rep, axis=1)
        twi = jnp.concatenate([tw_ref[1, q]] * rep, axis=1)
        o_ref[0, q] = (a * twr - b * twi).astype(o_ref.dtype)
        o_ref[1, q] = (a * twi + b * twr).astype(o_ref.dtype)


def _fft_stage1(u3, L1):
    B, L, C = u3.shape
    L2 = L // L1
    nl2 = 8
    k1 = np.arange(L1)
    ang1 = 2.0 * np.pi * np.outer(k1, np.arange(L1)) / L1
    w1 = jnp.asarray(np.concatenate([np.cos(ang1), -np.sin(ang1)], axis=0), F32).astype(BF16)
    angt = 2.0 * np.pi * np.outer(np.arange(L2), k1) / L
    tw = np.stack([np.cos(angt), -np.sin(angt)], axis=0)[..., None]
    tw = jnp.asarray(np.broadcast_to(tw, (2, L2, L1, LANES)), F32)
    uv = u3.reshape(B, L1, L2 * C)
    return pl.pallas_call(
        functools.partial(_fft1_kernel, L1=L1, nl2=nl2, C=C),
        grid=(B, L2 // nl2),
        in_specs=[pl.BlockSpec((2 * L1, L1), lambda b, j: (0, 0)),
                  pl.BlockSpec((2, nl2, L1, LANES), lambda b, j: (0, j, 0, 0)),
                  pl.BlockSpec((None, L1, nl2 * C), lambda b, j: (b, 0, j))],
        out_specs=pl.BlockSpec((None, 2, nl2, L1, C), lambda b, j: (b, 0, j, 0, 0)),
        out_shape=jax.ShapeDtypeStruct((B, 2, L2, L1, C), BF16),
        compiler_params=_cparams("parallel", "parallel"),
        name="fft_stage1",
    )(w1, tw, uv)


def _fft_stage2(g5):
    B, _, L2, L1, C = g5.shape
    ang = 2.0 * np.pi * np.outer(np.arange(L2), np.arange(L2)) / L2
    cs, sn = np.cos(ang), np.sin(ang)
    w2 = jnp.asarray(np.block([[cs, sn], [-sn, cs]]), F32).astype(BF16)
    N = L1 * C
    tn = _tile(N, 4096, LANES)
    gv = g5.reshape(B, 2 * L2, N)
    y = pl.pallas_call(
        _mm_kernel,
        grid=(B, N // tn),
        in_specs=[pl.BlockSpec((2 * L2, 2 * L2), lambda b, j: (0, 0)),
                  pl.BlockSpec((None, 2 * L2, tn), lambda b, j: (b, 0, j))],
        out_specs=pl.BlockSpec((None, 2 * L2, tn), lambda b, j: (b, 0, j)),
        out_shape=jax.ShapeDtypeStruct((B, 2 * L2, N), BF16),
        compiler_params=_cparams("parallel", "parallel"),
        name="fft_stage2",
    )(w2, gv)
    return y.reshape(B, 2, L2 * L1, C)


def _four_out_kernel(y_ref, cs_ref, wf_ref, g_ref, t_ref, o_ref, f_s, *, scale):
    @pl.when(pl.program_id(2) == 0)
    def _():
        yr = y_ref[0]
        yi = y_ref[1]
        cs = cs_ref[...]
        parts = []
        for g in range(FOUR_GROUPS):
            sl = slice(g * FOUR_GROUP_DIM, (g + 1) * FOUR_GROUP_DIM)
            lhs = jnp.concatenate([yr[:, sl], yi[:, sl]], axis=1)
            parts.append(jnp.dot(lhs, cs, preferred_element_type=F32))
        f_s[...] = (jnp.concatenate(parts, axis=1) * scale).astype(f_s.dtype)

    yf = jnp.dot(f_s[...], wf_ref[...], preferred_element_type=F32)
    o_ref[...] = (t_ref[...].astype(F32) + _sigmoid(g_ref[...].astype(F32)) * yf).astype(o_ref.dtype)


def _four_out(y4, w_four, proj, gate_off, t1):
    B, _, L, C = y4.shape
    D = w_four.shape[1]
    tm = _tile(L, 1024)
    tn = _tile(D, 2048, LANES, divides=(gate_off,))
    goff = gate_off // tn
    nti = L // tm
    ang = 2.0 * np.pi * np.outer(np.arange(FOUR_GROUP_DIM), np.arange(FOUR_GROUP_DIM)) / FOUR_GROUP_DIM
    cs = jnp.asarray(np.concatenate([np.cos(ang), np.sin(ang)], axis=0), F32).astype(BF16)
    scale = 1.0 / math.sqrt(L * FOUR_GROUP_DIM)
    return pl.pallas_call(
        functools.partial(_four_out_kernel, scale=scale),
        grid=(B, nti, D // tn),
        in_specs=[pl.BlockSpec((None, 2, tm, C), lambda b, i, j: (b, 0, i, 0)),
                  pl.BlockSpec((2 * FOUR_GROUP_DIM, FOUR_GROUP_DIM), lambda b, i, j: (0, 0)),
                  pl.BlockSpec((C, tn), lambda b, i, j: (0, j)),
                  pl.BlockSpec((tm, tn), lambda b, i, j: (b * nti + i, goff + j)),
                  pl.BlockSpec((tm, tn), lambda b, i, j: (b * nti + i, j))],
        out_specs=pl.BlockSpec((tm, tn), lambda b, i, j: (b * nti + i, j)),
        out_shape=jax.ShapeDtypeStruct((B * L, D), BF16),
        scratch_shapes=[pltpu.VMEM((tm, C), BF16)],
        compiler_params=_cparams("parallel", "parallel", "arbitrary"),
        name="four_out",
    )(y4, cs, w_four, proj, t1)


def _chan_dft_kernel(u_ref, cs_ref, o_ref):
    u = u_ref[...]
    cs = cs_ref[...]
    cos_parts, sin_parts = [], []
    for g in range(FOUR_GROUPS):
        r = jnp.dot(u[:, g * FOUR_GROUP_DIM:(g + 1) * FOUR_GROUP_DIM], cs, preferred_element_type=F32)
        cos_parts.append(r[:, :FOUR_GROUP_DIM])
        sin_parts.append(r[:, FOUR_GROUP_DIM:])
    o_ref[0] = jnp.concatenate(cos_parts, axis=1).astype(o_ref.dtype)
    o_ref[1] = jnp.concatenate(sin_parts, axis=1).astype(o_ref.dtype)


def _chan_dft(u, B, L):
    T, C = u.shape
    tm = _tile(L, 1024)
    nti = L // tm
    ang = 2.0 * np.pi * np.outer(np.arange(FOUR_GROUP_DIM), np.arange(FOUR_GROUP_DIM)) / FOUR_GROUP_DIM
    cs = jnp.asarray(np.concatenate([np.cos(ang), np.sin(ang)], axis=1), F32).astype(BF16)
    return pl.pallas_call(
        _chan_dft_kernel,
        grid=(B, nti),
        in_specs=[pl.BlockSpec((tm, C), lambda b, i: (b * nti + i, 0)),
                  pl.BlockSpec((FOUR_GROUP_DIM, 2 * FOUR_GROUP_DIM), lambda b, i: (0, 0))],
        out_specs=pl.BlockSpec((None, 2, tm, C), lambda b, i: (b, 0, i, 0)),
        out_shape=jax.ShapeDtypeStruct((B, 2, L, C), BF16),
        compiler_params=_cparams("parallel", "parallel"),
        name="chan_dft",
    )(u, cs)


def _dense_four_kernel(d_ref, z_ref, wf_ref, g_ref, t_ref, o_ref, acc_s, *, scale):
    k = pl.program_id(2)

    @pl.when(k == 0)
    def _():
        acc_s[...] = jnp.zeros_like(acc_s)

    acc_s[...] += jnp.dot(d_ref[...], z_ref[...], preferred_element_type=F32)

    @pl.when(k == pl.num_programs(2) - 1)
    def _():
        f = (acc_s[...] * scale).astype(BF16)
        yf = jnp.dot(f, wf_ref[...], preferred_element_type=F32)
        o_ref[...] = (t_ref[...].astype(F32) + _sigmoid(g_ref[...].astype(F32)) * yf).astype(o_ref.dtype)


def _dense_four_out(z4, w_four, proj, gate_off, t1):
    B, _, L, C = z4.shape
    D = w_four.shape[1]
    tm = _tile(L, 512)
    tk = _tile(2 * L, 2048, LANES)
    nti = L // tm
    assert gate_off % D == 0
    goff = gate_off // D
    prod = (jnp.arange(L, dtype=jnp.int32)[:, None] * jnp.arange(L, dtype=jnp.int32)[None, :]) % L
    ang = prod.astype(F32) * (2.0 * math.pi / L)
    dmat = jnp.concatenate([jnp.cos(ang), -jnp.sin(ang)], axis=1).astype(BF16)
    scale = 1.0 / math.sqrt(L * FOUR_GROUP_DIM)
    return pl.pallas_call(
        functools.partial(_dense_four_kernel, scale=scale),
        grid=(B, nti, (2 * L) // tk),
        in_specs=[pl.BlockSpec((tm, tk), lambda b, i, k: (i, k)),
                  pl.BlockSpec((None, tk, C), lambda b, i, k: (b, k, 0)),
                  pl.BlockSpec((C, D), lambda b, i, k: (0, 0)),
                  pl.BlockSpec((tm, D), lambda b, i, k: (b * nti + i, goff)),
                  pl.BlockSpec((tm, D), lambda b, i, k: (b * nti + i, 0))],
        out_specs=pl.BlockSpec((tm, D), lambda b, i, k: (b * nti + i, 0)),
        out_shape=jax.ShapeDtypeStruct((B * L, D), BF16),
        scratch_shapes=[pltpu.VMEM((tm, C), F32)],
        compiler_params=_cparams("parallel", "parallel", "arbitrary"),
        name="dense_four_out",
    )(dmat, z4.reshape(B, 2 * L, C), w_four, proj, t1)


def _ffn_kernel(te_ref, act_ref, x_ref, w1_ref, w3_ref, w2_ref, o_ref, acc_s):
    i = pl.program_id(0)
    j = pl.program_id(1)
    nj = pl.num_programs(1)

    @pl.when(j == 0)
    def _():
        acc_s[...] = jnp.zeros_like(acc_s)

    @pl.when(act_ref[i] > 0)
    def _():
        x = x_ref[...]
        h1 = jnp.dot(x, w1_ref[...], preferred_element_type=F32)
        h3 = jnp.dot(x, w3_ref[...], preferred_element_type=F32)
        h = (_silu(h1) * h3).astype(BF16)
        acc_s[...] += jnp.dot(h, w2_ref[...], preferred_element_type=F32)

    @pl.when(j == nj - 1)
    def _():
        o_ref[...] = acc_s[...].astype(o_ref.dtype)


def _ffn(x, w1, w3, w2, tile_expert, tile_active, tm):
    R, D = x.shape
    E, _, F = w1.shape
    tf = _tile(F, 512, LANES)
    nf = F // tf

    def wcol(i, j, te, act):
        return (te[i], 0, jnp.where(act[i] > 0, j, nf - 1))

    def wrow(i, j, te, act):
        return (te[i], jnp.where(act[i] > 0, j, nf - 1), 0)

    grid_spec = pltpu.PrefetchScalarGridSpec(
        num_scalar_prefetch=2,
        grid=(R // tm, nf),
        in_specs=[pl.BlockSpec((tm, D), lambda i, j, te, act: (i, 0)),
                  pl.BlockSpec((None, D, tf), wcol),
                  pl.BlockSpec((None, D, tf), wcol),
                  pl.BlockSpec((None, tf, D), wrow)],
        out_specs=pl.BlockSpec((tm, D), lambda i, j, te, act: (i, 0)),
        scratch_shapes=[pltpu.VMEM((tm, D), F32)],
    )
    return pl.pallas_call(
        _ffn_kernel,
        grid_spec=grid_spec,
        out_shape=jax.ShapeDtypeStruct((R, D), BF16),
        compiler_params=_cparams("parallel", "arbitrary"),
        name="ffn",
    )(tile_expert, tile_active, x, w1, w3, w2)


def _router_kernel(x_ref, r_ref, w_ref, i_ref, *, E):
    xh, xm, _ = _split3(x_ref[...])
    rh, rm, _ = _split3(r_ref[...])
    logits = (jnp.dot(xh, rh, preferred_element_type=F32) + jnp.dot(xm, rh, preferred_element_type=F32)
              + jnp.dot(xh, rm, preferred_element_type=F32))
    lane = lax.broadcasted_iota(jnp.int32, logits.shape, 1).astype(F32)
    logits = jnp.where(lane < E, logits, -jnp.inf)
    ex = jnp.exp(logits - jnp.max(logits, axis=-1, keepdims=True))
    probs = ex / jnp.sum(ex, axis=-1, keepdims=True)
    m1 = jnp.max(probs, axis=-1, keepdims=True)
    i1 = jnp.min(jnp.where(probs == m1, lane, float(LANES)), axis=-1, keepdims=True)
    rest = jnp.where(lane == i1, -1.0, probs)
    m2 = jnp.max(rest, axis=-1, keepdims=True)
    i2 = jnp.min(jnp.where(rest == m2, lane, float(LANES)), axis=-1, keepdims=True)
    den = m1 + m2
    w_ref[...] = jnp.where(lane == 0.0, m1 / den, jnp.where(lane == 1.0, m2 / den, 0.0))
    i_ref[...] = jnp.where(lane == 0.0, i1, jnp.where(lane == 1.0, i2, 0.0)).astype(jnp.int32)


def _router(x, router):
    T, D = x.shape
    E = router.shape[1]
    tm = _tile(T, 512)
    rp = jnp.zeros((D, LANES), F32).at[:, :E].set(router)
    row = pl.BlockSpec((tm, LANES), lambda i: (i, 0))
    return pl.pallas_call(
        functools.partial(_router_kernel, E=E),
        grid=(T // tm,),
        in_specs=[pl.BlockSpec((tm, D), lambda i: (i, 0)), pl.BlockSpec((D, LANES), lambda i: (0, 0))],
        out_specs=[row, row],
        out_shape=[jax.ShapeDtypeStruct((T, LANES), F32), jax.ShapeDtypeStruct((T, LANES), jnp.int32)],
        compiler_params=_cparams("parallel"),
        name="router",
    )(x, rp)


def _moe(xf, xb, router, we1, we3, we2, ln_g, ln_b):
    T, D = xf.shape
    E = we1.shape[0]
    tm = _tile(T, 1024)
    wt, idx = _router(xf, router)
    flat_e = idx[:, :TOP_K].reshape(-1)
    onehot = (flat_e[:, None] == jnp.arange(E, dtype=jnp.int32)[None, :]).astype(jnp.int32)
    rank = jnp.sum((jnp.cumsum(onehot, axis=0) - onehot) * onehot, axis=1)
    counts = jnp.sum(onehot, axis=0)
    padded = ((counts + tm - 1) // tm) * tm
    ends = jnp.cumsum(padded)
    pos = (ends - padded)[flat_e] + rank
    n_tiles = (TOP_K * T) // tm + E
    src = jnp.zeros((n_tiles * tm,), jnp.int32).at[pos].set(
        jnp.arange(TOP_K * T, dtype=jnp.int32) // TOP_K, unique_indices=True, mode="promise_in_bounds")
    starts = jnp.arange(n_tiles, dtype=jnp.int32) * tm
    tile_expert = jnp.minimum(jnp.searchsorted(ends, starts, side="right"), E - 1).astype(jnp.int32)
    tile_active = (starts < ends[-1]).astype(jnp.int32)
    x_sorted = xb.at[src].get(mode="promise_in_bounds")
    y_sorted = _ffn(x_sorted, we1, we3, we2, tile_expert, tile_active, tm)
    pos2 = pos.reshape(T, TOP_K)
    ya = y_sorted.at[pos2[:, 0]].get(mode="promise_in_bounds", unique_indices=True)
    yb = y_sorted.at[pos2[:, 1]].get(mode="promise_in_bounds", unique_indices=True)
    return _combine_ln(xf, ya, yb, wt, ln_g, ln_b)


def _prep_mixer(w_in, conv_w, conv_b, dt_bias_f, dt_bias_b, a_log_f, a_log_b, d_skip, gnorm_w,
                w_ssd_up, w_four, w_o):
    D = w_in.shape[0]
    DI = w_ssd_up.shape[0]
    CD = conv_b.shape[0]
    H = dt_bias_f.shape[0]
    DF = w_four.shape[0]
    G = SSD_GROUPS
    hpg = H // G
    o_dtf = DI + CD
    o_dtb = o_dtf + H
    o_u = o_dtb + H
    o_g = o_u + DF
    w_main = jnp.concatenate([w_in[:, :o_dtf], w_in[:, o_g:]], axis=1).astype(BF16)
    w_u = w_in[:, o_u:o_g].astype(BF16)

    def dt_rows(f, b, fill=0.0):
        lead = f.shape[:-1]
        out = jnp.full(lead + (G, DT_ROWS), fill, F32)
        out = out.at[..., :hpg].set(f.reshape(lead + (G, hpg)))
        return out.at[..., BWD_ROW0:BWD_ROW0 + hpg].set(b.reshape(lead + (G, hpg)))

    w_dt = dt_rows(w_in[:, o_dtf:o_dtb], w_in[:, o_dtb:o_u]).reshape(D, G * DT_ROWS).astype(BF16)
    dt_bias = dt_rows(dt_bias_f, dt_bias_b)
    a_log = dt_rows(a_log_f, a_log_b)
    a_log_l = jnp.zeros((G, 1, LANES), F32).at[:, 0, :DT_ROWS].set(a_log)
    return dict(
        w_main=w_main, w_udt=jnp.concatenate([w_u, w_dt], axis=1), w_u_cols=DF,
        conv_w=conv_w.reshape(CONV_K, CD), conv_b=conv_b.reshape(1, CD),
        dt_bias_c=dt_bias[..., None], a_log_c=a_log[..., None], a_log_l=a_log_l,
        d_skip_x=jnp.repeat(d_skip, DI // H).reshape(1, DI), gnorm_w=gnorm_w.reshape(1, DI),
        w_ssd_up=w_ssd_up.astype(BF16), w_four=w_four.astype(BF16), w_o=w_o.astype(BF16),
        DI=DI, P=DI // H, gate_off=o_dtf, D=D)


def _token_mixer(xf, xb, B, L, mp, ln_g, ln_b):
    T = B * L
    G = SSD_GROUPS
    DI, D = mp["DI"], mp["D"]
    proj = _mm(xb, mp["w_main"], BF16)
    u, dt = _mm_split(xb, mp["w_udt"], mp["w_u_cols"], BF16, F32)
    dt4 = dt.reshape(B, L, G, DT_ROWS).transpose(0, 2, 3, 1)
    v = _ssd(proj.reshape(B, L, -1), dt4, mp["conv_w"], mp["conv_b"], mp["dt_bias_c"], mp["a_log_c"],
             mp["a_log_l"], mp["d_skip_x"], mp["gnorm_w"], DI, mp["P"])
    t1 = _gate_mm(v.reshape(T, DI), mp["w_ssd_up"], proj, mp["gate_off"])
    if L <= DENSE_DFT_MAX_L:
        merged = _dense_four_out(_chan_dft(u, B, L), mp["w_four"], proj, mp["gate_off"] + D, t1)
    else:
        g5 = _fft_stage1(u.reshape(B, L, -1), L // FFT_L2)
        y4 = _fft_stage2(g5)
        merged = _four_out(y4, mp["w_four"], proj, mp["gate_off"] + D, t1)
    return _mm_res_ln(merged, mp["w_o"], xf, ln_g, ln_b)


def kernel(x_prompt, x_sample, ln_in_g, ln_in_b, w_in_0, conv_w_0, conv_b_0, dt_bias_f_0, dt_bias_b_0, a_log_f_0, a_log_b_0, d_skip_0, gnorm_w_0, w_ssd_up_0, w_four_0, w_o_0, ln1_g_0, ln1_b_0, w1_0, w3_0, w2_0, ln2_g_0, ln2_b_0, w_in_1, conv_w_1, conv_b_1, dt_bias_f_1, dt_bias_b_1, a_log_f_1, a_log_b_1, d_skip_1, gnorm_w_1, w_ssd_up_1, w_four_1, w_o_1, ln1_g_1, ln1_b_1, router_1, we1_1, we3_1, we2_1, ln2_g_1, ln2_b_1):
    mp0 = _prep_mixer(w_in_0, conv_w_0, conv_b_0, dt_bias_f_0, dt_bias_b_0, a_log_f_0, a_log_b_0,
                      d_skip_0, gnorm_w_0, w_ssd_up_0, w_four_0, w_o_0)
    mp1 = _prep_mixer(w_in_1, conv_w_1, conv_b_1, dt_bias_f_1, dt_bias_b_1, a_log_f_1, a_log_b_1,
                      d_skip_1, gnorm_w_1, w_ssd_up_1, w_four_1, w_o_1)
    w1 = w1_0.astype(BF16)[None]
    w3 = w3_0.astype(BF16)[None]
    w2 = w2_0.astype(BF16)[None]
    we1, we3, we2, x_prompt = lax.optimization_barrier(
        (we1_1.astype(BF16), we3_1.astype(BF16), we2_1.astype(BF16), x_prompt))

    def trunk(x):
        B, L, D = x.shape
        T = B * L
        xf, xb = _ln_in(x.reshape(T, D), ln_in_g, ln_in_b)
        xf, xb = _token_mixer(xf, xb, B, L, mp0, ln1_g_0, ln1_b_0)
        tm = _tile(T, 1024)
        n_tiles = T // tm
        ffn = _ffn(xb, w1, w3, w2, jnp.zeros((n_tiles,), jnp.int32), jnp.ones((n_tiles,), jnp.int32), tm)
        xf, xb = _res_ln(xf, ffn, ln2_g_0, ln2_b_0)
        xf, xb = _token_mixer(xf, xb, B, L, mp1, ln1_g_1, ln1_b_1)
        xf, _ = _moe(xf, xb, router_1, we1, we3, we2, ln2_g_1, ln2_b_1)
        return xf.reshape(B, L, D)

    return (trunk(x_prompt), trunk(x_sample))
```

```python
import functools
import math

import numpy as np
import jax
import jax.numpy as jnp
from jax import lax
from jax.experimental import pallas as pl
from jax.experimental.pallas import tpu as pltpu

F32 = jnp.float32
BF16 = jnp.bfloat16

SSD_GROUPS = 8
D_STATE = 128
CONV_K = 5
CHUNK = 128
FOUR_GROUPS = 8
FOUR_GROUP_DIM = 128
TOP_K = 2
DEPTH = 2
ALPHA = (2 * DEPTH) ** 0.25
LN_EPS = 1e-5
RMS_EPS = 1e-5
LOG2E = math.log2(math.e)

LANES = 128
BF16_SUBLANES = 16
VMEM_LIMIT_BYTES = 56 * 1024 * 1024

DT_ROWS = 16
BWD_ROW0 = 8
FFT_L2 = 128
DENSE_DFT_MAX_L = 2048


def _cparams(*sem):
    return pltpu.CompilerParams(dimension_semantics=sem, vmem_limit_bytes=VMEM_LIMIT_BYTES)


def _tile(n, pref, align=8, divides=()):
    t = min(pref, n)
    t -= t % align
    while t > align:
        if n % t == 0 and all(d % t == 0 for d in divides):
            return t
        t -= align
    return align


def _sigmoid(x):
    return 0.5 + 0.5 * jnp.tanh(0.5 * x)


def _silu(x):
    hx = 0.5 * x
    return hx + hx * jnp.tanh(hx)


def _ln_rows(v, g, b):
    mu = jnp.mean(v, axis=-1, keepdims=True)
    d = v - mu
    var = jnp.mean(d * d, axis=-1, keepdims=True)
    return d * lax.rsqrt(var + LN_EPS) * g + b


def _split3(x):
    hi = x.astype(BF16)
    r1 = x - hi.astype(F32)
    mid = r1.astype(BF16)
    lo = (r1 - mid.astype(F32)).astype(BF16)
    return hi, mid, lo


def _ln_in_kernel(x_ref, g_ref, b_ref, xf_ref, xb_ref):
    y = _ln_rows(x_ref[...], g_ref[...], b_ref[...])
    xf_ref[...] = y
    xb_ref[...] = y.astype(BF16)


def _ln_in(x, g, b):
    T, D = x.shape
    tm = _tile(T, 512)
    row = pl.BlockSpec((tm, D), lambda i: (i, 0))
    vec = pl.BlockSpec((1, D), lambda i: (0, 0))
    return pl.pallas_call(
        _ln_in_kernel,
        grid=(T // tm,),
        in_specs=[row, vec, vec],
        out_specs=[row, row],
        out_shape=[jax.ShapeDtypeStruct((T, D), F32), jax.ShapeDtypeStruct((T, D), BF16)],
        compiler_params=_cparams("parallel"),
        name="ln_in",
    )(x, g.reshape(1, D), b.reshape(1, D))


def _res_ln_kernel(x_ref, y_ref, g_ref, b_ref, xf_ref, xb_ref):
    v = ALPHA * x_ref[...] + y_ref[...].astype(F32)
    y = _ln_rows(v, g_ref[...], b_ref[...])
    xf_ref[...] = y
    xb_ref[...] = y.astype(BF16)


def _res_ln(x, y, g, b):
    T, D = x.shape
    tm = _tile(T, 512)
    row = pl.BlockSpec((tm, D), lambda i: (i, 0))
    vec = pl.BlockSpec((1, D), lambda i: (0, 0))
    return pl.pallas_call(
        _res_ln_kernel,
        grid=(T // tm,),
        in_specs=[row, row, vec, vec],
        out_specs=[row, row],
        out_shape=[jax.ShapeDtypeStruct((T, D), F32), jax.ShapeDtypeStruct((T, D), BF16)],
        compiler_params=_cparams("parallel"),
        name="res_ln",
    )(x, y, g.reshape(1, D), b.reshape(1, D))


def _combine_ln_kernel(x_ref, ya_ref, yb_ref, w_ref, g_ref, b_ref, xf_ref, xb_ref):
    w = w_ref[...]
    mix = w[:, 0:1] * ya_ref[...].astype(F32) + w[:, 1:2] * yb_ref[...].astype(F32)
    y = _ln_rows(ALPHA * x_ref[...] + mix, g_ref[...], b_ref[...])
    xf_ref[...] = y
    xb_ref[...] = y.astype(BF16)


def _combine_ln(x, ya, yb, w, g, b):
    T, D = x.shape
    tm = _tile(T, 512)
    row = pl.BlockSpec((tm, D), lambda i: (i, 0))
    wsp = pl.BlockSpec((tm, LANES), lambda i: (i, 0))
    vec = pl.BlockSpec((1, D), lambda i: (0, 0))
    return pl.pallas_call(
        _combine_ln_kernel,
        grid=(T // tm,),
        in_specs=[row, row, row, wsp, vec, vec],
        out_specs=[row, row],
        out_shape=[jax.ShapeDtypeStruct((T, D), F32), jax.ShapeDtypeStruct((T, D), BF16)],
        compiler_params=_cparams("parallel"),
        name="combine_ln",
    )(x, ya, yb, w, g.reshape(1, D), b.reshape(1, D))


def _mm_kernel(a_ref, w_ref, o_ref):
    o_ref[...] = jnp.dot(a_ref[...], w_ref[...], preferred_element_type=F32).astype(o_ref.dtype)


def _mm(a, w, out_dtype, tm_pref=1024, tn_pref=1024):
    M, K = a.shape
    N = w.shape[1]
    tm = _tile(M, tm_pref)
    tn = _tile(N, tn_pref, LANES)
    return pl.pallas_call(
        _mm_kernel,
        grid=(M // tm, N // tn),
        in_specs=[pl.BlockSpec((tm, K), lambda i, j: (i, 0)), pl.BlockSpec((K, tn), lambda i, j: (0, j))],
        out_specs=pl.BlockSpec((tm, tn), lambda i, j: (i, j)),
        out_shape=jax.ShapeDtypeStruct((M, N), out_dtype),
        compiler_params=_cparams("parallel", "arbitrary"),
        name="mm",
    )(a, w)


def _mm_split_kernel(a_ref, w_ref, o1_ref, o2_ref):
    r = jnp.dot(a_ref[...], w_ref[...], preferred_element_type=F32)
    n1 = o1_ref.shape[1]
    o1_ref[...] = r[:, :n1].astype(o1_ref.dtype)
    o2_ref[...] = r[:, n1:].astype(o2_ref.dtype)


def _mm_split(a, w, n1, dtype1, dtype2):
    M, K = a.shape
    N = w.shape[1]
    tm = _tile(M, 1024)
    return pl.pallas_call(
        _mm_split_kernel,
        grid=(M // tm,),
        in_specs=[pl.BlockSpec((tm, K), lambda i: (i, 0)), pl.BlockSpec((K, N), lambda i: (0, 0))],
        out_specs=[pl.BlockSpec((tm, n1), lambda i: (i, 0)), pl.BlockSpec((tm, N - n1), lambda i: (i, 0))],
        out_shape=[jax.ShapeDtypeStruct((M, n1), dtype1), jax.ShapeDtypeStruct((M, N - n1), dtype2)],
        compiler_params=_cparams("parallel"),
        name="mm_split",
    )(a, w)


def _gate_mm_kernel(a_ref, w_ref, g_ref, o_ref):
    y = jnp.dot(a_ref[...], w_ref[...], preferred_element_type=F32)
    o_ref[...] = (_sigmoid(g_ref[...].astype(F32)) * y).astype(o_ref.dtype)


def _gate_mm(a, w, proj, gate_off):
    M, K = a.shape
    N = w.shape[1]
    tm = _tile(M, 1024)
    tn = _tile(N, 1024, LANES, divides=(gate_off,))
    goff = gate_off // tn
    return pl.pallas_call(
        _gate_mm_kernel,
        grid=(M // tm, N // tn),
        in_specs=[pl.BlockSpec((tm, K), lambda i, j: (i, 0)),
                  pl.BlockSpec((K, tn), lambda i, j: (0, j)),
                  pl.BlockSpec((tm, tn), lambda i, j: (i, goff + j))],
        out_specs=pl.BlockSpec((tm, tn), lambda i, j: (i, j)),
        out_shape=jax.ShapeDtypeStruct((M, N), BF16),
        compiler_params=_cparams("parallel", "arbitrary"),
        name="gate_mm",
    )(a, w, proj)


def _mm_res_ln_kernel(a_ref, w_ref, x_ref, g_ref, b_ref, xf_ref, xb_ref):
    y = jnp.dot(a_ref[...], w_ref[...], preferred_element_type=F32)
    out = _ln_rows(ALPHA * x_ref[...] + y, g_ref[...], b_ref[...])
    xf_ref[...] = out
    xb_ref[...] = out.astype(BF16)


def _mm_res_ln(a, w, x, g, b):
    M, K = a.shape
    D = w.shape[1]
    tm = _tile(M, 512)
    row = pl.BlockSpec((tm, D), lambda i: (i, 0))
    vec = pl.BlockSpec((1, D), lambda i: (0, 0))
    return pl.pallas_call(
        _mm_res_ln_kernel,
        grid=(M // tm,),
        in_specs=[pl.BlockSpec((tm, K), lambda i: (i, 0)), pl.BlockSpec((K, D), lambda i: (0, 0)), row, vec, vec],
        out_specs=[row, row],
        out_shape=[jax.ShapeDtypeStruct((M, D), F32), jax.ShapeDtypeStruct((M, D), BF16)],
        compiler_params=_cparams("parallel"),
        name="mm_res_ln",
    )(a, w, x, g.reshape(1, D), b.reshape(1, D))


PACK_DT = 0
PACK_STATE = DT_ROWS
PACK_OFF = 2 * DT_ROWS
PACK_E = 3 * DT_ROWS


def _conv_silu_chunk(src_ref, w, bias, c, nc, L):
    halo = BF16_SUBLANES
    n_ext = CHUNK + 2 * halo
    W = src_ref.shape[1]
    s0 = pl.multiple_of(c * CHUNK, CHUNK)
    sp = pl.multiple_of(jnp.maximum(s0 - halo, 0), halo)
    sn = pl.multiple_of(jnp.minimum(s0 + CHUNK, L - halo), halo)
    cur = src_ref[pl.ds(s0, CHUNK), :].astype(F32)
    prev = jnp.where(c > 0, src_ref[pl.ds(sp, halo), :].astype(F32), 0.0)
    nxt = jnp.where(c < nc - 1, src_ref[pl.ds(sn, halo), :].astype(F32), 0.0)
    ext = jnp.concatenate([prev, cur, nxt], axis=0)
    acc = jnp.broadcast_to(bias, (CHUNK, W))
    for k in range(CONV_K):
        shift = (n_ext - halo - (k - CONV_K // 2)) % n_ext
        acc = acc + w[k:k + 1, :] * pltpu.roll(ext, shift, 0)[:CHUNK]
    return _silu(acc)


def _ssd_kernel(z_ref, xs_ref, b_ref, c_ref, dt_ref, cwx_ref, cwb_ref, cwc_ref, cbx_ref, cbb_ref, cbc_ref,
                dtb_ref, alc_ref, all_ref, dsk_ref, gnw_ref, o_ref,
                xs_s, c_s, bt_s, cb_s, pack_s, erow_s, yb_s, stf_s, stb_s, *, L, P):
    nc = L // CHUNK
    half = nc // 2
    gw = xs_s.shape[1]

    ri = lax.broadcasted_iota(jnp.int32, (CHUNK, CHUNK), 0)
    ci = lax.broadcasted_iota(jnp.int32, (CHUNK, CHUNK), 1)
    incl_upper = (ri <= ci).astype(BF16)
    row16 = lax.broadcasted_iota(jnp.int32, (DT_ROWS, CHUNK), 0)
    zpad = jnp.zeros((CHUNK - DT_ROWS, CHUNK), F32)
    a_c = -jnp.exp(alc_ref[...])
    a_l = -jnp.exp(all_ref[...])
    dt_bias = dtb_ref[...]
    wx, wb, wc = cwx_ref[...], cwb_ref[...], cwc_ref[...]
    bx, bb, bc_bias = cbx_ref[...], cbb_ref[...], cbc_ref[...]

    def prepass(c, carry):
        s0 = pl.multiple_of(c * CHUNK, CHUNK)
        xs_s[pl.ds(s0, CHUNK), :] = _conv_silu_chunk(xs_ref, wx, bx, c, nc, L).astype(BF16)
        bcv = _conv_silu_chunk(b_ref, wb, bb, c, nc, L).astype(BF16)
        ccv = _conv_silu_chunk(c_ref, wc, bc_bias, c, nc, L).astype(BF16)
        c_s[pl.ds(s0, CHUNK), :] = ccv
        bt_s[:, pl.ds(s0, CHUNK)] = bcv.astype(F32).T.astype(BF16)
        cb = lax.dot_general(ccv, bcv, (((1,), (1,)), ((), ())), preferred_element_type=F32)
        cb_s[pl.ds(s0, CHUNK), :] = cb.astype(BF16)
        raw = dt_ref[:, pl.ds(s0, CHUNK)] + dt_bias
        dtv = jnp.maximum(raw, 0.0) + jnp.log1p(jnp.exp(-jnp.abs(raw)))
        a_row = dtv * a_c
        hi, mid, lo = _split3(a_row)
        acum_row = (jnp.dot(hi, incl_upper, preferred_element_type=F32)
                    + jnp.dot(mid, incl_upper, preferred_element_type=F32)
                    + jnp.dot(lo, incl_upper, preferred_element_type=F32))
        erow_s[:, pl.ds(s0, CHUNK)] = LOG2E * jnp.where(row16 < BWD_ROW0, acum_row, acum_row - a_row)
        acum_col = jnp.concatenate([acum_row, zpad], axis=0).T
        dt_col = jnp.concatenate([dtv, zpad], axis=0).T
        total = acum_col[CHUNK - 1:CHUNK, :]
        fwd_lane = ci < BWD_ROW0
        e_bwd = acum_col - dt_col * a_l
        e_col = jnp.where(fwd_lane, acum_col, e_bwd)
        f_state = dt_col * jnp.exp(jnp.where(fwd_lane, total - acum_col, e_bwd))
        f_off = jnp.exp(jnp.where(fwd_lane, acum_col, total - e_bwd))
        pack = jnp.where(ci < PACK_STATE, dt_col,
                         jnp.where(ci < PACK_OFF, pltpu.roll(f_state, PACK_STATE, 1),
                                   jnp.where(ci < PACK_E, pltpu.roll(f_off, PACK_OFF, 1),
                                             jnp.where(ci < PACK_E + DT_ROWS, pltpu.roll(LOG2E * e_col, PACK_E, 1), 0.0))))
        pack_s[pl.ds(s0, CHUNK), :] = pack
        return carry

    prepass(0, 0)
    prepass(nc - 1, 0)

    def expand_matrix(row0):
        ej = lax.broadcasted_iota(jnp.int32, (CHUNK, gw), 0)
        ec = lax.broadcasted_iota(jnp.int32, (CHUNK, gw), 1)
        blocks = []
        for f in (PACK_DT, PACK_STATE, PACK_OFF):
            lo = (ej - f - row0) * P
            blocks.append(((ec >= lo) & (ec < lo + P)).astype(BF16))
        return jnp.concatenate(blocks, axis=1)

    expand_f = expand_matrix(0)
    expand_b = expand_matrix(BWD_ROW0)
    mask_f = ci <= ri
    mask_b = ci >= ri
    stf_s[...] = jnp.zeros_like(stf_s)
    stb_s[...] = jnp.zeros_like(stb_s)

    def scan_chunk(c, bwd):
        row0 = BWD_ROW0 if bwd else 0
        st_ref = stb_s if bwd else stf_s
        s0 = pl.multiple_of(c * CHUNK, CHUNK)
        pk = pack_s[pl.ds(s0, CHUNK), :]
        ex = jnp.dot(pk.astype(BF16), expand_b if bwd else expand_f, preferred_element_type=F32)
        dt_x = ex[:, :gw]
        fs_x = ex[:, gw:2 * gw]
        fo_x = ex[:, 2 * gw:]
        xc = xs_s[pl.ds(s0, CHUNK), :]
        x_dt = xc * dt_x.astype(BF16)
        x_st = xc * fs_x.astype(BF16)
        cc = c_s[pl.ds(s0, CHUNK), :]
        cbm = jnp.where(mask_b if bwd else mask_f, cb_s[pl.ds(s0, CHUNK), :].astype(F32), 0.0)
        state = st_ref[...]
        y = jnp.dot(cc, state.astype(BF16), preferred_element_type=F32) * fo_x
        er = erow_s[:, pl.ds(s0, CHUNK)]
        per_tile = LANES // P
        parts = []
        for q in range(gw // LANES):
            tile = x_dt[:, q * LANES:(q + 1) * LANES]
            acc = None
            for r in range(per_tile):
                j = row0 + q * per_tile + r
                col = pk[:, PACK_E + j:PACK_E + j + 1]
                row = er[j:j + 1, :]
                seg = (row - col) if bwd else (col - row)
                lm = (jnp.exp2(jnp.minimum(seg, 0.0)) * cbm).astype(BF16)
                rhs = jnp.where((ci >= r * P) & (ci < (r + 1) * P), tile, jnp.zeros_like(tile))
                d = jnp.dot(lm, rhs, preferred_element_type=F32)
                acc = d if acc is None else acc + d
            parts.append(acc)
        y = y + jnp.concatenate(parts, axis=1)
        e_total = fo_x[0:1, :] if bwd else fo_x[CHUNK - 1:CHUNK, :]
        st_ref[...] = state * e_total + jnp.dot(bt_s[:, pl.ds(s0, CHUNK)], x_st, preferred_element_type=F32)
        return y, xc

    def finish(y, xc, c):
        s0 = pl.multiple_of(c * CHUNK, CHUNK)
        v = (y + xc.astype(F32) * dsk_ref[...]) * _silu(z_ref[pl.ds(s0, CHUNK), :].astype(F32))
        ms = jnp.mean(v * v, axis=-1, keepdims=True)
        o_ref[pl.ds(s0, CHUNK), :] = (v * lax.rsqrt(ms + RMS_EPS) * gnw_ref[...]).astype(o_ref.dtype)

    def first_half(i, carry, prepare_next=True):
        cf = i
        cb_ = nc - 1 - i
        yf, _ = scan_chunk(cf, False)
        yb, _ = scan_chunk(cb_, True)
        o_ref[pl.ds(pl.multiple_of(cf * CHUNK, CHUNK), CHUNK), :] = yf.astype(o_ref.dtype)
        yb_s[pl.ds(pl.multiple_of((cb_ - half) * CHUNK, CHUNK), CHUNK), :] = yb.astype(yb_s.dtype)
        if prepare_next:
            prepass(cf + 1, 0)
            prepass(cb_ - 1, 0)
        return carry

    def second_half(i, carry):
        cf = i
        cb_ = nc - 1 - i
        yf, xf = scan_chunk(cf, False)
        yb, xb = scan_chunk(cb_, True)
        yb_prev = yb_s[pl.ds(pl.multiple_of((cf - half) * CHUNK, CHUNK), CHUNK), :].astype(F32)
        yf_prev = o_ref[pl.ds(pl.multiple_of(cb_ * CHUNK, CHUNK), CHUNK), :].astype(F32)
        finish(yf + yb_prev, xf, cf)
        finish(yb + yf_prev, xb, cb_)
        return carry

    lax.fori_loop(0, half - 1, first_half, 0)
    first_half(half - 1, 0, prepare_next=False)
    lax.fori_loop(half, nc, second_half, 0, unroll=4)


def _ssd(proj3, dt4, conv_w, conv_b, dt_bias_c, a_log_c, a_log_l, d_skip_x, gnorm_w, DI, P):
    B, L, _ = proj3.shape
    G = SSD_GROUPS
    gw = DI // G
    assert D_STATE == LANES and L % (2 * CHUNK) == 0 and gw % LANES == 0 and LANES % P == 0 and gw // P <= BWD_ROW0
    xoff = DI // gw
    boff = 2 * DI // LANES
    coff = boff + G
    cwb_off = DI // LANES
    io_bytes = L * (3 * gw + 2 * D_STATE) * 2
    scratch_bytes = L * (gw + 3 * D_STATE) * 2 + L * LANES * 4 + (L // 2) * gw * 2
    single = 2 * io_bytes + scratch_bytes > (VMEM_LIMIT_BYTES * 3) // 4

    def big(shape, imap):
        if single:
            return pl.BlockSpec(shape, imap, pipeline_mode=pl.Buffered(1))
        return pl.BlockSpec(shape, imap)

    in_specs = [
        big((None, L, gw), lambda b, g: (b, 0, g)),
        big((None, L, gw), lambda b, g: (b, 0, xoff + g)),
        big((None, L, D_STATE), lambda b, g: (b, 0, boff + g)),
        big((None, L, D_STATE), lambda b, g: (b, 0, coff + g)),
        pl.BlockSpec((None, None, DT_ROWS, L), lambda b, g: (b, g, 0, 0)),
        pl.BlockSpec((CONV_K, gw), lambda b, g: (0, g)),
        pl.BlockSpec((CONV_K, D_STATE), lambda b, g: (0, cwb_off + g)),
        pl.BlockSpec((CONV_K, D_STATE), lambda b, g: (0, cwb_off + G + g)),
        pl.BlockSpec((1, gw), lambda b, g: (0, g)),
        pl.BlockSpec((1, D_STATE), lambda b, g: (0, cwb_off + g)),
        pl.BlockSpec((1, D_STATE), lambda b, g: (0, cwb_off + G + g)),
        pl.BlockSpec((None, DT_ROWS, 1), lambda b, g: (g, 0, 0)),
        pl.BlockSpec((None, DT_ROWS, 1), lambda b, g: (g, 0, 0)),
        pl.BlockSpec((None, 1, LANES), lambda b, g: (g, 0, 0)),
        pl.BlockSpec((1, gw), lambda b, g: (0, g)),
        pl.BlockSpec((1, gw), lambda b, g: (0, g)),
    ]
    scratch = [
        pltpu.VMEM((L, gw), BF16),
        pltpu.VMEM((L, D_STATE), BF16),
        pltpu.VMEM((D_STATE, L), BF16),
        pltpu.VMEM((L, CHUNK), BF16),
        pltpu.VMEM((L, LANES), F32),
        pltpu.VMEM((DT_ROWS, L), F32),
        pltpu.VMEM((L // 2, gw), BF16),
        pltpu.VMEM((D_STATE, gw), F32),
        pltpu.VMEM((D_STATE, gw), F32),
    ]
    return pl.pallas_call(
        functools.partial(_ssd_kernel, L=L, P=P),
        grid=(B, G),
        in_specs=in_specs,
        out_specs=big((None, L, gw), lambda b, g: (b, 0, g)),
        out_shape=jax.ShapeDtypeStruct((B, L, DI), BF16),
        scratch_shapes=scratch,
        compiler_params=_cparams("parallel", "parallel"),
        name="ssd",
    )(proj3, proj3, proj3, proj3, dt4, conv_w, conv_w, conv_w, conv_b, conv_b, conv_b,
      dt_bias_c, a_log_c, a_log_l, d_skip_x, gnorm_w)


def _fft1_kernel(w_ref, tw_ref, u_ref, o_ref, *, L1, nl2, C):
    res = jnp.dot(w_ref[...], u_ref[...], preferred_element_type=F32)
    rep = C // LANES
    for q in range(nl2):
        a = res[:L1, q * C:(q + 1) * C]
        b = res[L1:, q * C:(q + 1) * C]
        twr = jnp.concatenate([tw_ref[0, q]] * rep, axis=1)
        twi = jnp.concatenate([tw_ref[1, q]] * rep, axis=1)
        o_ref[0, q] = (a * twr - b * twi).astype(o_ref.dtype)
        o_ref[1, q] = (a * twi + b * twr).astype(o_ref.dtype)


def _fft_stage1(u3, L1):
    B, L, C = u3.shape
    L2 = L // L1
    nl2 = 8
    k1 = np.arange(L1)
    ang1 = 2.0 * np.pi * np.outer(k1, np.arange(L1)) / L1
    w1 = jnp.asarray(np.concatenate([np.cos(ang1), -np.sin(ang1)], axis=0), F32).astype(BF16)
    angt = 2.0 * np.pi * np.outer(np.arange(L2), k1) / L
    tw = np.stack([np.cos(angt), -np.sin(angt)], axis=0)[..., None]
    tw = jnp.asarray(np.broadcast_to(tw, (2, L2, L1, LANES)), F32)
    uv = u3.reshape(B, L1, L2 * C)
    return pl.pallas_call(
        functools.partial(_fft1_kernel, L1=L1, nl2=nl2, C=C),
        grid=(B, L2 // nl2),
        in_specs=[pl.BlockSpec((2 * L1, L1), lambda b, j: (0, 0)),
                  pl.BlockSpec((2, nl2, L1, LANES), lambda b, j: (0, j, 0, 0)),
                  pl.BlockSpec((None, L1, nl2 * C), lambda b, j: (b, 0, j))],
        out_specs=pl.BlockSpec((None, 2, nl2, L1, C), lambda b, j: (b, 0, j, 0, 0)),
        out_shape=jax.ShapeDtypeStruct((B, 2, L2, L1, C), BF16),
        compiler_params=_cparams("parallel", "parallel"),
        name="fft_stage1",
    )(w1, tw, uv)


def _fft_stage2(g5):
    B, _, L2, L1, C = g5.shape
    ang = 2.0 * np.pi * np.outer(np.arange(L2), np.arange(L2)) / L2
    cs, sn = np.cos(ang), np.sin(ang)
    w2 = jnp.asarray(np.block([[cs, sn], [-sn, cs]]), F32).astype(BF16)
    N = L1 * C
    tn = _tile(N, 4096, LANES)
    gv = g5.reshape(B, 2 * L2, N)
    y = pl.pallas_call(
        _mm_kernel,
        grid=(B, N // tn),
        in_specs=[pl.BlockSpec((2 * L2, 2 * L2), lambda b, j: (0, 0)),
                  pl.BlockSpec((None, 2 * L2, tn), lambda b, j: (b, 0, j))],
        out_specs=pl.BlockSpec((None, 2 * L2, tn), lambda b, j: (b, 0, j)),
        out_shape=jax.ShapeDtypeStruct((B, 2 * L2, N), BF16),
        compiler_params=_cparams("parallel", "parallel"),
        name="fft_stage2",
    )(w2, gv)
    return y.reshape(B, 2, L2 * L1, C)


def _four_out_kernel(y_ref, cs_ref, wf_ref, g_ref, t_ref, o_ref, f_s, *, scale):
    @pl.when(pl.program_id(2) == 0)
    def _():
        yr = y_ref[0]
        yi = y_ref[1]
        cs = cs_ref[...]
        parts = []
        for g in range(FOUR_GROUPS):
            sl = slice(g * FOUR_GROUP_DIM, (g + 1) * FOUR_GROUP_DIM)
            lhs = jnp.concatenate([yr[:, sl], yi[:, sl]], axis=1)
            parts.append(jnp.dot(lhs, cs, preferred_element_type=F32))
        f_s[...] = (jnp.concatenate(parts, axis=1) * scale).astype(f_s.dtype)

    yf = jnp.dot(f_s[...], wf_ref[...], preferred_element_type=F32)
    o_ref[...] = (t_ref[...].astype(F32) + _sigmoid(g_ref[...].astype(F32)) * yf).astype(o_ref.dtype)


def _four_out(y4, w_four, proj, gate_off, t1):
    B, _, L, C = y4.shape
    D = w_four.shape[1]
    tm = _tile(L, 1024)
    tn = _tile(D, 2048, LANES, divides=(gate_off,))
    goff = gate_off // tn
    nti = L // tm
    ang = 2.0 * np.pi * np.outer(np.arange(FOUR_GROUP_DIM), np.arange(FOUR_GROUP_DIM)) / FOUR_GROUP_DIM
    cs = jnp.asarray(np.concatenate([np.cos(ang), np.sin(ang)], axis=0), F32).astype(BF16)
    scale = 1.0 / math.sqrt(L * FOUR_GROUP_DIM)
    return pl.pallas_call(
        functools.partial(_four_out_kernel, scale=scale),
        grid=(B, nti, D // tn),
        in_specs=[pl.BlockSpec((None, 2, tm, C), lambda b, i, j: (b, 0, i, 0)),
                  pl.BlockSpec((2 * FOUR_GROUP_DIM, FOUR_GROUP_DIM), lambda b, i, j: (0, 0)),
                  pl.BlockSpec((C, tn), lambda b, i, j: (0, j)),
                  pl.BlockSpec((tm, tn), lambda b, i, j: (b * nti + i, goff + j)),
                  pl.BlockSpec((tm, tn), lambda b, i, j: (b * nti + i, j))],
        out_specs=pl.BlockSpec((tm, tn), lambda b, i, j: (b * nti + i, j)),
        out_shape=jax.ShapeDtypeStruct((B * L, D), BF16),
        scratch_shapes=[pltpu.VMEM((tm, C), BF16)],
        compiler_params=_cparams("parallel", "parallel", "arbitrary"),
        name="four_out",
    )(y4, cs, w_four, proj, t1)


def _chan_dft_kernel(u_ref, cs_ref, o_ref):
    u = u_ref[...]
    cs = cs_ref[...]
    cos_parts, sin_parts = [], []
    for g in range(FOUR_GROUPS):
        r = jnp.dot(u[:, g * FOUR_GROUP_DIM:(g + 1) * FOUR_GROUP_DIM], cs, preferred_element_type=F32)
        cos_parts.append(r[:, :FOUR_GROUP_DIM])
        sin_parts.append(r[:, FOUR_GROUP_DIM:])
    o_ref[0] = jnp.concatenate(cos_parts, axis=1).astype(o_ref.dtype)
    o_ref[1] = jnp.concatenate(sin_parts, axis=1).astype(o_ref.dtype)


def _chan_dft(u, B, L):
    T, C = u.shape
    tm = _tile(L, 1024)
    nti = L // tm
    ang = 2.0 * np.pi * np.outer(np.arange(FOUR_GROUP_DIM), np.arange(FOUR_GROUP_DIM)) / FOUR_GROUP_DIM
    cs = jnp.asarray(np.concatenate([np.cos(ang), np.sin(ang)], axis=1), F32).astype(BF16)
    return pl.pallas_call(
        _chan_dft_kernel,
        grid=(B, nti),
        in_specs=[pl.BlockSpec((tm, C), lambda b, i: (b * nti + i, 0)),
                  pl.BlockSpec((FOUR_GROUP_DIM, 2 * FOUR_GROUP_DIM), lambda b, i: (0, 0))],
        out_specs=pl.BlockSpec((None, 2, tm, C), lambda b, i: (b, 0, i, 0)),
        out_shape=jax.ShapeDtypeStruct((B, 2, L, C), BF16),
        compiler_params=_cparams("parallel", "parallel"),
        name="chan_dft",
    )(u, cs)


def _dense_four_kernel(d_ref, z_ref, wf_ref, g_ref, t_ref, o_ref, acc_s, *, scale):
    k = pl.program_id(2)

    @pl.when(k == 0)
    def _():
        acc_s[...] = jnp.zeros_like(acc_s)

    acc_s[...] += jnp.dot(d_ref[...], z_ref[...], preferred_element_type=F32)

    @pl.when(k == pl.num_programs(2) - 1)
    def _():
        f = (acc_s[...] * scale).astype(BF16)
        yf = jnp.dot(f, wf_ref[...], preferred_element_type=F32)
        o_ref[...] = (t_ref[...].astype(F32) + _sigmoid(g_ref[...].astype(F32)) * yf).astype(o_ref.dtype)


def _dense_four_out(z4, w_four, proj, gate_off, t1):
    B, _, L, C = z4.shape
    D = w_four.shape[1]
    tm = _tile(L, 512)
    tk = _tile(2 * L, 2048, LANES)
    nti = L // tm
    assert gate_off % D == 0
    goff = gate_off // D
    prod = (jnp.arange(L, dtype=jnp.int32)[:, None] * jnp.arange(L, dtype=jnp.int32)[None, :]) % L
    ang = prod.astype(F32) * (2.0 * math.pi / L)
    dmat = jnp.concatenate([jnp.cos(ang), -jnp.sin(ang)], axis=1).astype(BF16)
    scale = 1.0 / math.sqrt(L * FOUR_GROUP_DIM)
    return pl.pallas_call(
        functools.partial(_dense_four_kernel, scale=scale),
        grid=(B, nti, (2 * L) // tk),
        in_specs=[pl.BlockSpec((tm, tk), lambda b, i, k: (i, k)),
                  pl.BlockSpec((None, tk, C), lambda b, i, k: (b, k, 0)),
                  pl.BlockSpec((C, D), lambda b, i, k: (0, 0)),
                  pl.BlockSpec((tm, D), lambda b, i, k: (b * nti + i, goff)),
                  pl.BlockSpec((tm, D), lambda b, i, k: (b * nti + i, 0))],
        out_specs=pl.BlockSpec((tm, D), lambda b, i, k: (b * nti + i, 0)),
        out_shape=jax.ShapeDtypeStruct((B * L, D), BF16),
        scratch_shapes=[pltpu.VMEM((tm, C), F32)],
        compiler_params=_cparams("parallel", "parallel", "arbitrary"),
        name="dense_four_out",
    )(dmat, z4.reshape(B, 2 * L, C), w_four, proj, t1)


def _ffn_kernel(te_ref, act_ref, x_ref, w1_ref, w3_ref, w2_ref, o_ref, acc_s):
    i = pl.program_id(0)
    j = pl.program_id(1)
    nj = pl.num_programs(1)

    @pl.when(j == 0)
    def _():
        acc_s[...] = jnp.zeros_like(acc_s)

    @pl.when(act_ref[i] > 0)
    def _():
        x = x_ref[...]
        h1 = jnp.dot(x, w1_ref[...], preferred_element_type=F32)
        h3 = jnp.dot(x, w3_ref[...], preferred_element_type=F32)
        h = (_silu(h1) * h3).astype(BF16)
        acc_s[...] += jnp.dot(h, w2_ref[...], preferred_element_type=F32)

    @pl.when(j == nj - 1)
    def _():
        o_ref[...] = acc_s[...].astype(o_ref.dtype)


def _ffn(x, w1, w3, w2, tile_expert, tile_active, tm):
    R, D = x.shape
    E, _, F = w1.shape
    tf = _tile(F, 512, LANES)
    nf = F // tf

    def wcol(i, j, te, act):
        return (te[i], 0, jnp.where(act[i] > 0, j, nf - 1))

    def wrow(i, j, te, act):
        return (te[i], jnp.where(act[i] > 0, j, nf - 1), 0)

    grid_spec = pltpu.PrefetchScalarGridSpec(
        num_scalar_prefetch=2,
        grid=(R // tm, nf),
        in_specs=[pl.BlockSpec((tm, D), lambda i, j, te, act: (i, 0)),
                  pl.BlockSpec((None, D, tf), wcol),
                  pl.BlockSpec((None, D, tf), wcol),
                  pl.BlockSpec((None, tf, D), wrow)],
        out_specs=pl.BlockSpec((tm, D), lambda i, j, te, act: (i, 0)),
        scratch_shapes=[pltpu.VMEM((tm, D), F32)],
    )
    return pl.pallas_call(
        _ffn_kernel,
        grid_spec=grid_spec,
        out_shape=jax.ShapeDtypeStruct((R, D), BF16),
        compiler_params=_cparams("parallel", "arbitrary"),
        name="ffn",
    )(tile_expert, tile_active, x, w1, w3, w2)


def _router_kernel(x_ref, r_ref, w_ref, i_ref, *, E):
    xh, xm, _ = _split3(x_ref[...])
    rh, rm, _ = _split3(r_ref[...])
    logits = (jnp.dot(xh, rh, preferred_element_type=F32) + jnp.dot(xm, rh, preferred_element_type=F32)
              + jnp.dot(xh, rm, preferred_element_type=F32))
    lane = lax.broadcasted_iota(jnp.int32, logits.shape, 1).astype(F32)
    logits = jnp.where(lane < E, logits, -jnp.inf)
    ex = jnp.exp(logits - jnp.max(logits, axis=-1, keepdims=True))
    probs = ex / jnp.sum(ex, axis=-1, keepdims=True)
    m1 = jnp.max(probs, axis=-1, keepdims=True)
    i1 = jnp.min(jnp.where(probs == m1, lane, float(LANES)), axis=-1, keepdims=True)
    rest = jnp.where(lane == i1, -1.0, probs)
    m2 = jnp.max(rest, axis=-1, keepdims=True)
    i2 = jnp.min(jnp.where(rest == m2, lane, float(LANES)), axis=-1, keepdims=True)
    den = m1 + m2
    w_ref[...] = jnp.where(lane == 0.0, m1 / den, jnp.where(lane == 1.0, m2 / den, 0.0))
    i_ref[...] = jnp.where(lane == 0.0, i1, jnp.where(lane == 1.0, i2, 0.0)).astype(jnp.int32)


def _router(x, router):
    T, D = x.shape
    E = router.shape[1]
    tm = _tile(T, 512)
    rp = jnp.zeros((D, LANES), F32).at[:, :E].set(router)
    row = pl.BlockSpec((tm, LANES), lambda i: (i, 0))
    return pl.pallas_call(
        functools.partial(_router_kernel, E=E),
        grid=(T // tm,),
        in_specs=[pl.BlockSpec((tm, D), lambda i: (i, 0)), pl.BlockSpec((D, LANES), lambda i: (0, 0))],
        out_specs=[row, row],
        out_shape=[jax.ShapeDtypeStruct((T, LANES), F32), jax.ShapeDtypeStruct((T, LANES), jnp.int32)],
        compiler_params=_cparams("parallel"),
        name="router",
    )(x, rp)


def _moe(xf, xb, router, we1, we3, we2, ln_g, ln_b):
    T, D = xf.shape
    E = we1.shape[0]
    tm = _tile(T, 1024)
    wt, idx = _router(xf, router)
    flat_e = idx[:, :TOP_K].reshape(-1)
    onehot = (flat_e[:, None] == jnp.arange(E, dtype=jnp.int32)[None, :]).astype(jnp.int32)
    rank = jnp.sum((jnp.cumsum(onehot, axis=0) - onehot) * onehot, axis=1)
    counts = jnp.sum(onehot, axis=0)
    padded = ((counts + tm - 1) // tm) * tm
    ends = jnp.cumsum(padded)
    pos = (ends - padded)[flat_e] + rank
    n_tiles = (TOP_K * T) // tm + E
    src = jnp.zeros((n_tiles * tm,), jnp.int32).at[pos].set(
        jnp.arange(TOP_K * T, dtype=jnp.int32) // TOP_K, unique_indices=True, mode="promise_in_bounds")
    starts = jnp.arange(n_tiles, dtype=jnp.int32) * tm
    tile_expert = jnp.minimum(jnp.searchsorted(ends, starts, side="right"), E - 1).astype(jnp.int32)
    tile_active = (starts < ends[-1]).astype(jnp.int32)
    x_sorted = xb.at[src].get(mode="promise_in_bounds")
    y_sorted = _ffn(x_sorted, we1, we3, we2, tile_expert, tile_active, tm)
    pos2 = pos.reshape(T, TOP_K)
    ya = y_sorted.at[pos2[:, 0]].get(mode="promise_in_bounds", unique_indices=True)
    yb = y_sorted.at[pos2[:, 1]].get(mode="promise_in_bounds", unique_indices=True)
    return _combine_ln(xf, ya, yb, wt, ln_g, ln_b)


def _prep_mixer(w_in, conv_w, conv_b, dt_bias_f, dt_bias_b, a_log_f, a_log_b, d_skip, gnorm_w,
                w_ssd_up, w_four, w_o):
    D = w_in.shape[0]
    DI = w_ssd_up.shape[0]
    CD = conv_b.shape[0]
    H = dt_bias_f.shape[0]
    DF = w_four.shape[0]
    G = SSD_GROUPS
    hpg = H // G
    o_dtf = DI + CD
    o_dtb = o_dtf + H
    o_u = o_dtb + H
    o_g = o_u + DF
    w_main = jnp.concatenate([w_in[:, :o_dtf], w_in[:, o_g:]], axis=1).astype(BF16)
    w_u = w_in[:, o_u:o_g].astype(BF16)

    def dt_rows(f, b, fill=0.0):
        lead = f.shape[:-1]
        out = jnp.full(lead + (G, DT_ROWS), fill, F32)
        out = out.at[..., :hpg].set(f.reshape(lead + (G, hpg)))
        return out.at[..., BWD_ROW0:BWD_ROW0 + hpg].set(b.reshape(lead + (G, hpg)))

    w_dt = dt_rows(w_in[:, o_dtf:o_dtb], w_in[:, o_dtb:o_u]).reshape(D, G * DT_ROWS).astype(BF16)
    dt_bias = dt_rows(dt_bias_f, dt_bias_b)
    a_log = dt_rows(a_log_f, a_log_b)
    a_log_l = jnp.zeros((G, 1, LANES), F32).at[:, 0, :DT_ROWS].set(a_log)
    return dict(
        w_main=w_main, w_udt=jnp.concatenate([w_u, w_dt], axis=1), w_u_cols=DF,
        conv_w=conv_w.reshape(CONV_K, CD), conv_b=conv_b.reshape(1, CD),
        dt_bias_c=dt_bias[..., None], a_log_c=a_log[..., None], a_log_l=a_log_l,
        d_skip_x=jnp.repeat(d_skip, DI // H).reshape(1, DI), gnorm_w=gnorm_w.reshape(1, DI),
        w_ssd_up=w_ssd_up.astype(BF16), w_four=w_four.astype(BF16), w_o=w_o.astype(BF16),
        DI=DI, P=DI // H, gate_off=o_dtf, D=D)


def _token_mixer(xf, xb, B, L, mp, ln_g, ln_b):
    T = B * L
    G = SSD_GROUPS
    DI, D = mp["DI"], mp["D"]
    proj = _mm(xb, mp["w_main"], BF16)
    u, dt = _mm_split(xb, mp["w_udt"], mp["w_u_cols"], BF16, F32)
    dt4 = dt.reshape(B, L, G, DT_ROWS).transpose(0, 2, 3, 1)
    v = _ssd(proj.reshape(B, L, -1), dt4, mp["conv_w"], mp["conv_b"], mp["dt_bias_c"], mp["a_log_c"],
             mp["a_log_l"], mp["d_skip_x"], mp["gnorm_w"], DI, mp["P"])
    t1 = _gate_mm(v.reshape(T, DI), mp["w_ssd_up"], proj, mp["gate_off"])
    if L <= DENSE_DFT_MAX_L:
        merged = _dense_four_out(_chan_dft(u, B, L), mp["w_four"], proj, mp["gate_off"] + D, t1)
    else:
        g5 = _fft_stage1(u.reshape(B, L, -1), L // FFT_L2)
        y4 = _fft_stage2(g5)
        merged = _four_out(y4, mp["w_four"], proj, mp["gate_off"] + D, t1)
    return _mm_res_ln(merged, mp["w_o"], xf, ln_g, ln_b)


def kernel(x_prompt, x_sample, ln_in_g, ln_in_b, w_in_0, conv_w_0, conv_b_0, dt_bias_f_0, dt_bias_b_0, a_log_f_0, a_log_b_0, d_skip_0, gnorm_w_0, w_ssd_up_0, w_four_0, w_o_0, ln1_g_0, ln1_b_0, w1_0, w3_0, w2_0, ln2_g_0, ln2_b_0, w_in_1, conv_w_1, conv_b_1, dt_bias_f_1, dt_bias_b_1, a_log_f_1, a_log_b_1, d_skip_1, gnorm_w_1, w_ssd_up_1, w_four_1, w_o_1, ln1_g_1, ln1_b_1, router_1, we1_1, we3_1, we2_1, ln2_g_1, ln2_b_1):
    mp0 = _prep_mixer(w_in_0, conv_w_0, conv_b_0, dt_bias_f_0, dt_bias_b_0, a_log_f_0, a_log_b_0,
                      d_skip_0, gnorm_w_0, w_ssd_up_0, w_four_0, w_o_0)
    mp1 = _prep_mixer(w_in_1, conv_w_1, conv_b_1, dt_bias_f_1, dt_bias_b_1, a_log_f_1, a_log_b_1,
                      d_skip_1, gnorm_w_1, w_ssd_up_1, w_four_1, w_o_1)
    w1 = w1_0.astype(BF16)[None]
    w3 = w3_0.astype(BF16)[None]
    w2 = w2_0.astype(BF16)[None]
    we1, we3, we2, x_prompt = lax.optimization_barrier(
        (we1_1.astype(BF16), we3_1.astype(BF16), we2_1.astype(BF16), x_prompt))

    def trunk(x):
        B, L, D = x.shape
        T = B * L
        xf, xb = _ln_in(x.reshape(T, D), ln_in_g, ln_in_b)
        xf, xb = _token_mixer(xf, xb, B, L, mp0, ln1_g_0, ln1_b_0)
        tm = _tile(T, 1024)
        n_tiles = T // tm
        ffn = _ffn(xb, w1, w3, w2, jnp.zeros((n_tiles,), jnp.int32), jnp.ones((n_tiles,), jnp.int32), tm)
        xf, xb = _res_ln(xf, ffn, ln2_g_0, ln2_b_0)
        xf, xb = _token_mixer(xf, xb, B, L, mp1, ln1_g_1, ln1_b_1)
        xf, _ = _moe(xf, xb, router_1, we1, we3, we2, ln2_g_1, ln2_b_1)
        return xf.reshape(B, L, D)

    return (trunk(x_prompt), trunk(x_sample))
```

```python
import functools
import math

import numpy as np
import jax
import jax.numpy as jnp
from jax import lax
from jax.experimental import pallas as pl
from jax.experimental.pallas import tpu as pltpu

F32 = jnp.float32
BF16 = jnp.bfloat16

SSD_GROUPS = 8
D_STATE = 128
CONV_K = 5
CHUNK = 128
FOUR_GROUPS = 8
FOUR_GROUP_DIM = 128
TOP_K = 2
DEPTH = 2
ALPHA = (2 * DEPTH) ** 0.25
LN_EPS = 1e-5
RMS_EPS = 1e-5
LOG2E = math.log2(math.e)

LANES = 128
BF16_SUBLANES = 16
VMEM_LIMIT_BYTES = 58 * 1024 * 1024

DT_ROWS = 16
BWD_ROW0 = 8
FFT_L2 = 128
DENSE_DFT_MAX_L = 2048


def _cparams(*sem):
    return pltpu.CompilerParams(dimension_semantics=sem, vmem_limit_bytes=VMEM_LIMIT_BYTES)


def _tile(n, pref, align=8, divides=()):
    t = min(pref, n)
    t -= t % align
    while t > align:
        if n % t == 0 and all(d % t == 0 for d in divides):
            return t
        t -= align
    return align


def _sigmoid(x):
    return 0.5 + 0.5 * jnp.tanh(0.5 * x)


def _silu(x):
    hx = 0.5 * x
    return hx + hx * jnp.tanh(hx)


def _ln_rows(v, g, b):
    mu = jnp.mean(v, axis=-1, keepdims=True)
    d = v - mu
    var = jnp.mean(d * d, axis=-1, keepdims=True)
    return d * lax.rsqrt(var + LN_EPS) * g + b


def _split3(x):
    hi = x.astype(BF16)
    r1 = x - hi.astype(F32)
    mid = r1.astype(BF16)
    lo = (r1 - mid.astype(F32)).astype(BF16)
    return hi, mid, lo


def _ln_in_kernel(x_ref, g_ref, b_ref, xf_ref, xb_ref):
    y = _ln_rows(x_ref[...], g_ref[...], b_ref[...])
    xf_ref[...] = y
    xb_ref[...] = y.astype(BF16)


def _ln_in(x, g, b):
    T, D = x.shape
    tm = _tile(T, 512)
    row = pl.BlockSpec((tm, D), lambda i: (i, 0))
    vec = pl.BlockSpec((1, D), lambda i: (0, 0))
    return pl.pallas_call(
        _ln_in_kernel,
        grid=(T // tm,),
        in_specs=[row, vec, vec],
        out_specs=[row, row],
        out_shape=[jax.ShapeDtypeStruct((T, D), F32), jax.ShapeDtypeStruct((T, D), BF16)],
        compiler_params=_cparams("parallel"),
        name="ln_in",
    )(x, g.reshape(1, D), b.reshape(1, D))


def _res_ln_kernel(x_ref, y_ref, g_ref, b_ref, xf_ref, xb_ref):
    v = ALPHA * x_ref[...] + y_ref[...].astype(F32)
    y = _ln_rows(v, g_ref[...], b_ref[...])
    xf_ref[...] = y
    xb_ref[...] = y.astype(BF16)


def _res_ln(x, y, g, b):
    T, D = x.shape
    tm = _tile(T, 512)
    row = pl.BlockSpec((tm, D), lambda i: (i, 0))
    vec = pl.BlockSpec((1, D), lambda i: (0, 0))
    return pl.pallas_call(
        _res_ln_kernel,
        grid=(T // tm,),
        in_specs=[row, row, vec, vec],
        out_specs=[row, row],
        out_shape=[jax.ShapeDtypeStruct((T, D), F32), jax.ShapeDtypeStruct((T, D), BF16)],
        compiler_params=_cparams("parallel"),
        name="res_ln",
    )(x, y, g.reshape(1, D), b.reshape(1, D))


def _combine_ln_kernel(x_ref, ya_ref, yb_ref, w_ref, g_ref, b_ref, xf_ref, xb_ref):
    w = w_ref[...]
    mix = w[:, 0:1] * ya_ref[...].astype(F32) + w[:, 1:2] * yb_ref[...].astype(F32)
    y = _ln_rows(ALPHA * x_ref[...] + mix, g_ref[...], b_ref[...])
    xf_ref[...] = y
    xb_ref[...] = y.astype(BF16)


def _combine_ln(x, ya, yb, w, g, b):
    T, D = x.shape
    tm = _tile(T, 512)
    row = pl.BlockSpec((tm, D), lambda i: (i, 0))
    wsp = pl.BlockSpec((tm, LANES), lambda i: (i, 0))
    vec = pl.BlockSpec((1, D), lambda i: (0, 0))
    return pl.pallas_call(
        _combine_ln_kernel,
        grid=(T // tm,),
        in_specs=[row, row, row, wsp, vec, vec],
        out_specs=[row, row],
        out_shape=[jax.ShapeDtypeStruct((T, D), F32), jax.ShapeDtypeStruct((T, D), BF16)],
        compiler_params=_cparams("parallel"),
        name="combine_ln",
    )(x, ya, yb, w, g.reshape(1, D), b.reshape(1, D))


def _mm_kernel(a_ref, w_ref, o_ref):
    o_ref[...] = jnp.dot(a_ref[...], w_ref[...], preferred_element_type=F32).astype(o_ref.dtype)


def _mm(a, w, out_dtype, tm_pref=1024, tn_pref=1024):
    M, K = a.shape
    N = w.shape[1]
    tm = _tile(M, tm_pref)
    tn = _tile(N, tn_pref, LANES)
    return pl.pallas_call(
        _mm_kernel,
        grid=(M // tm, N // tn),
        in_specs=[pl.BlockSpec((tm, K), lambda i, j: (i, 0)), pl.BlockSpec((K, tn), lambda i, j: (0, j))],
        out_specs=pl.BlockSpec((tm, tn), lambda i, j: (i, j)),
        out_shape=jax.ShapeDtypeStruct((M, N), out_dtype),
        compiler_params=_cparams("parallel", "arbitrary"),
        name="mm",
    )(a, w)


def _mm_split_kernel(a_ref, w_ref, o1_ref, o2_ref):
    r = jnp.dot(a_ref[...], w_ref[...], preferred_element_type=F32)
    n1 = o1_ref.shape[1]
    o1_ref[...] = r[:, :n1].astype(o1_ref.dtype)
    o2_ref[...] = r[:, n1:].astype(o2_ref.dtype)


def _mm_split(a, w, n1, dtype1, dtype2):
    M, K = a.shape
    N = w.shape[1]
    tm = _tile(M, 1024)
    return pl.pallas_call(
        _mm_split_kernel,
        grid=(M // tm,),
        in_specs=[pl.BlockSpec((tm, K), lambda i: (i, 0)), pl.BlockSpec((K, N), lambda i: (0, 0))],
        out_specs=[pl.BlockSpec((tm, n1), lambda i: (i, 0)), pl.BlockSpec((tm, N - n1), lambda i: (i, 0))],
        out_shape=[jax.ShapeDtypeStruct((M, n1), dtype1), jax.ShapeDtypeStruct((M, N - n1), dtype2)],
        compiler_params=_cparams("parallel"),
        name="mm_split",
    )(a, w)


def _gate_mm_kernel(a_ref, w_ref, g_ref, o_ref):
    y = jnp.dot(a_ref[...], w_ref[...], preferred_element_type=F32)
    o_ref[...] = (_sigmoid(g_ref[...].astype(F32)) * y).astype(o_ref.dtype)


def _gate_mm(a, w, proj, gate_off):
    M, K = a.shape
    N = w.shape[1]
    tm = _tile(M, 1024)
    tn = _tile(N, 1024, LANES, divides=(gate_off,))
    goff = gate_off // tn
    return pl.pallas_call(
        _gate_mm_kernel,
        grid=(M // tm, N // tn),
        in_specs=[pl.BlockSpec((tm, K), lambda i, j: (i, 0)),
                  pl.BlockSpec((K, tn), lambda i, j: (0, j)),
                  pl.BlockSpec((tm, tn), lambda i, j: (i, goff + j))],
        out_specs=pl.BlockSpec((tm, tn), lambda i, j: (i, j)),
        out_shape=jax.ShapeDtypeStruct((M, N), BF16),
        compiler_params=_cparams("parallel", "arbitrary"),
        name="gate_mm",
    )(a, w, proj)


def _mm_res_ln_kernel(a_ref, w_ref, x_ref, g_ref, b_ref, xf_ref, xb_ref):
    y = jnp.dot(a_ref[...], w_ref[...], preferred_element_type=F32)
    out = _ln_rows(ALPHA * x_ref[...] + y, g_ref[...], b_ref[...])
    xf_ref[...] = out
    xb_ref[...] = out.astype(BF16)


def _mm_res_ln(a, w, x, g, b):
    M, K = a.shape
    D = w.shape[1]
    tm = _tile(M, 512)
    row = pl.BlockSpec((tm, D), lambda i: (i, 0))
    vec = pl.BlockSpec((1, D), lambda i: (0, 0))
    return pl.pallas_call(
        _mm_res_ln_kernel,
        grid=(M // tm,),
        in_specs=[pl.BlockSpec((tm, K), lambda i: (i, 0)), pl.BlockSpec((K, D), lambda i: (0, 0)), row, vec, vec],
        out_specs=[row, row],
        out_shape=[jax.ShapeDtypeStruct((M, D), F32), jax.ShapeDtypeStruct((M, D), BF16)],
        compiler_params=_cparams("parallel"),
        name="mm_res_ln",
    )(a, w, x, g.reshape(1, D), b.reshape(1, D))


PACK_DT = 0
PACK_STATE = DT_ROWS
PACK_OFF = 2 * DT_ROWS
PACK_E = 3 * DT_ROWS


def _conv_silu_chunk(src_ref, w, bias, c, nc, L):
    halo = BF16_SUBLANES
    n_ext = CHUNK + 2 * halo
    W = src_ref.shape[1]
    s0 = pl.multiple_of(c * CHUNK, CHUNK)
    sp = pl.multiple_of(jnp.maximum(s0 - halo, 0), halo)
    sn = pl.multiple_of(jnp.minimum(s0 + CHUNK, L - halo), halo)
    cur = src_ref[pl.ds(s0, CHUNK), :].astype(F32)
    prev = jnp.where(c > 0, src_ref[pl.ds(sp, halo), :].astype(F32), 0.0)
    nxt = jnp.where(c < nc - 1, src_ref[pl.ds(sn, halo), :].astype(F32), 0.0)
    ext = jnp.concatenate([prev, cur, nxt], axis=0)
    acc = jnp.broadcast_to(bias, (CHUNK, W))
    for k in range(CONV_K):
        shift = (n_ext - halo - (k - CONV_K // 2)) % n_ext
        acc = acc + w[k:k + 1, :] * pltpu.roll(ext, shift, 0)[:CHUNK]
    return _silu(acc)


def _ssd_kernel(z_ref, xs_ref, b_ref, c_ref, dt_ref, cwx_ref, cwb_ref, cwc_ref, cbx_ref, cbb_ref, cbc_ref,
                dtb_ref, alc_ref, all_ref, dsk_ref, gnw_ref, o_ref,
                xs_s, c_s, bt_s, cb_s, pack_s, erow_s, yb_s, stf_s, stb_s, *, L, P):
    nc = L // CHUNK
    half = nc // 2
    gw = xs_s.shape[1]

    ri = lax.broadcasted_iota(jnp.int32, (CHUNK, CHUNK), 0)
    ci = lax.broadcasted_iota(jnp.int32, (CHUNK, CHUNK), 1)
    incl_upper = (ri <= ci).astype(BF16)
    row16 = lax.broadcasted_iota(jnp.int32, (DT_ROWS, CHUNK), 0)
    zpad = jnp.zeros((CHUNK - DT_ROWS, CHUNK), F32)
    a_c = -jnp.exp(alc_ref[...])
    a_l = -jnp.exp(all_ref[...])
    dt_bias = dtb_ref[...]
    wx, wb, wc = cwx_ref[...], cwb_ref[...], cwc_ref[...]
    bx, bb, bc_bias = cbx_ref[...], cbb_ref[...], cbc_ref[...]

    def prepass(c, carry):
        s0 = pl.multiple_of(c * CHUNK, CHUNK)
        xs_s[pl.ds(s0, CHUNK), :] = _conv_silu_chunk(xs_ref, wx, bx, c, nc, L).astype(BF16)
        bcv = _conv_silu_chunk(b_ref, wb, bb, c, nc, L).astype(BF16)
        ccv = _conv_silu_chunk(c_ref, wc, bc_bias, c, nc, L).astype(BF16)
        c_s[pl.ds(s0, CHUNK), :] = ccv
        bt_s[:, pl.ds(s0, CHUNK)] = bcv.astype(F32).T.astype(BF16)
        cb = lax.dot_general(ccv, bcv, (((1,), (1,)), ((), ())), preferred_element_type=F32)
        cb_s[pl.ds(s0, CHUNK), :] = cb.astype(BF16)
        raw = dt_ref[:, pl.ds(s0, CHUNK)] + dt_bias
        dtv = jnp.maximum(raw, 0.0) + jnp.log1p(jnp.exp(-jnp.abs(raw)))
        a_row = dtv * a_c
        hi, mid, lo = _split3(a_row)
        acum_row = (jnp.dot(hi, incl_upper, preferred_element_type=F32)
                    + jnp.dot(mid, incl_upper, preferred_element_type=F32)
                    + jnp.dot(lo, incl_upper, preferred_element_type=F32))
        erow_s[:, pl.ds(s0, CHUNK)] = LOG2E * jnp.where(row16 < BWD_ROW0, acum_row, acum_row - a_row)
        acum_col = jnp.concatenate([acum_row, zpad], axis=0).T
        dt_col = jnp.concatenate([dtv, zpad], axis=0).T
        total = acum_col[CHUNK - 1:CHUNK, :]
        fwd_lane = ci < BWD_ROW0
        e_bwd = acum_col - dt_col * a_l
        e_col = jnp.where(fwd_lane, acum_col, e_bwd)
        f_state = dt_col * jnp.exp(jnp.where(fwd_lane, total - acum_col, e_bwd))
        f_off = jnp.exp(jnp.where(fwd_lane, acum_col, total - e_bwd))
        pack = jnp.where(ci < PACK_STATE, dt_col,
                         jnp.where(ci < PACK_OFF, pltpu.roll(f_state, PACK_STATE, 1),
                                   jnp.where(ci < PACK_E, pltpu.roll(f_off, PACK_OFF, 1),
                                             jnp.where(ci < PACK_E + DT_ROWS, pltpu.roll(LOG2E * e_col, PACK_E, 1), 0.0))))
        pack_s[pl.ds(s0, CHUNK), :] = pack
        return carry

    prepass(0, 0)
    prepass(nc - 1, 0)

    def expand_matrix(row0):
        ej = lax.broadcasted_iota(jnp.int32, (CHUNK, gw), 0)
        ec = lax.broadcasted_iota(jnp.int32, (CHUNK, gw), 1)
        blocks = []
        for f in (PACK_DT, PACK_STATE, PACK_OFF):
            lo = (ej - f - row0) * P
            blocks.append(((ec >= lo) & (ec < lo + P)).astype(BF16))
        return jnp.concatenate(blocks, axis=1)

    expand_f = expand_matrix(0)
    expand_b = expand_matrix(BWD_ROW0)
    mask_f = ci <= ri
    mask_b = ci >= ri
    stf_s[...] = jnp.zeros_like(stf_s)
    stb_s[...] = jnp.zeros_like(stb_s)

    def scan_chunk(c, bwd):
        row0 = BWD_ROW0 if bwd else 0
        st_ref = stb_s if bwd else stf_s
        s0 = pl.multiple_of(c * CHUNK, CHUNK)
        pk = pack_s[pl.ds(s0, CHUNK), :]
        ex = jnp.dot(pk.astype(BF16), expand_b if bwd else expand_f, preferred_element_type=F32)
        dt_x = ex[:, :gw]
        fs_x = ex[:, gw:2 * gw]
        fo_x = ex[:, 2 * gw:]
        xc = xs_s[pl.ds(s0, CHUNK), :]
        x_dt = xc * dt_x.astype(BF16)
        x_st = xc * fs_x.astype(BF16)
        cc = c_s[pl.ds(s0, CHUNK), :]
        cbm = jnp.where(mask_b if bwd else mask_f, cb_s[pl.ds(s0, CHUNK), :].astype(F32), 0.0)
        state = st_ref[...]
        y = jnp.dot(cc, state.astype(BF16), preferred_element_type=F32) * fo_x
        er = erow_s[:, pl.ds(s0, CHUNK)]
        per_tile = LANES // P
        parts = []
        for q in range(gw // LANES):
            tile = x_dt[:, q * LANES:(q + 1) * LANES]
            acc = None
            for r in range(per_tile):
                j = row0 + q * per_tile + r
                col = pk[:, PACK_E + j:PACK_E + j + 1]
                row = er[j:j + 1, :]
                seg = (row - col) if bwd else (col - row)
                lm = (jnp.exp2(jnp.minimum(seg, 0.0)) * cbm).astype(BF16)
                rhs = jnp.where((ci >= r * P) & (ci < (r + 1) * P), tile, jnp.zeros_like(tile))
                d = jnp.dot(lm, rhs, preferred_element_type=F32)
                acc = d if acc is None else acc + d
            parts.append(acc)
        y = y + jnp.concatenate(parts, axis=1)
        e_total = fo_x[0:1, :] if bwd else fo_x[CHUNK - 1:CHUNK, :]
        st_ref[...] = state * e_total + jnp.dot(bt_s[:, pl.ds(s0, CHUNK)], x_st, preferred_element_type=F32)
        return y, xc

    def finish(y, xc, c):
        s0 = pl.multiple_of(c * CHUNK, CHUNK)
        v = (y + xc.astype(F32) * dsk_ref[...]) * _silu(z_ref[pl.ds(s0, CHUNK), :].astype(F32))
        ms = jnp.mean(v * v, axis=-1, keepdims=True)
        o_ref[pl.ds(s0, CHUNK), :] = (v * lax.rsqrt(ms + RMS_EPS) * gnw_ref[...]).astype(o_ref.dtype)

    def first_half(i, carry, prepare_next=True):
        cf = i
        cb_ = nc - 1 - i
        yf, _ = scan_chunk(cf, False)
        yb, _ = scan_chunk(cb_, True)
        o_ref[pl.ds(pl.multiple_of(cf * CHUNK, CHUNK), CHUNK), :] = yf.astype(o_ref.dtype)
        yb_s[pl.ds(pl.multiple_of((cb_ - half) * CHUNK, CHUNK), CHUNK), :] = yb.astype(yb_s.dtype)
        if prepare_next:
            prepass(cf + 1, 0)
            prepass(cb_ - 1, 0)
        return carry

    def second_half(i, carry):
        cf = i
        cb_ = nc - 1 - i
        yf, xf = scan_chunk(cf, False)
        yb, xb = scan_chunk(cb_, True)
        yb_prev = yb_s[pl.ds(pl.multiple_of((cf - half) * CHUNK, CHUNK), CHUNK), :].astype(F32)
        yf_prev = o_ref[pl.ds(pl.multiple_of(cb_ * CHUNK, CHUNK), CHUNK), :].astype(F32)
        finish(yf + yb_prev, xf, cf)
        finish(yb + yf_prev, xb, cb_)
        return carry

    lax.fori_loop(0, half - 1, first_half, 0)
    first_half(half - 1, 0, prepare_next=False)
    lax.fori_loop(half, nc, second_half, 0, unroll=4)


def _ssd(proj3, dt4, conv_w, conv_b, dt_bias_c, a_log_c, a_log_l, d_skip_x, gnorm_w, DI, P):
    B, L, _ = proj3.shape
    G = SSD_GROUPS
    gw = DI // G
    assert D_STATE == LANES and L % (2 * CHUNK) == 0 and gw % LANES == 0 and LANES % P == 0 and gw // P <= BWD_ROW0
    xoff = DI // gw
    boff = 2 * DI // LANES
    coff = boff + G
    cwb_off = DI // LANES
    seq_bytes = L * 2
    scratch_bytes = seq_bytes * (gw + 3 * D_STATE) + L * LANES * 4 + (L // 2) * gw * 2
    conv_in_bytes = seq_bytes * (gw + 2 * D_STATE)
    late_bytes = seq_bytes * 2 * gw
    budget = (VMEM_LIMIT_BYTES * 31) // 32
    single_late = 2 * (conv_in_bytes + late_bytes) + scratch_bytes > (VMEM_LIMIT_BYTES * 3) // 4
    single_conv = single_late and 2 * conv_in_bytes + late_bytes + scratch_bytes > budget

    def spec(shape, imap, single):
        if single:
            return pl.BlockSpec(shape, imap, pipeline_mode=pl.Buffered(1))
        return pl.BlockSpec(shape, imap)

    def big(shape, imap):
        return spec(shape, imap, single_conv)

    def late(shape, imap):
        return spec(shape, imap, single_late)

    in_specs = [
        late((None, L, gw), lambda b, g: (b, 0, g)),
        big((None, L, gw), lambda b, g: (b, 0, xoff + g)),
        big((None, L, D_STATE), lambda b, g: (b, 0, boff + g)),
        big((None, L, D_STATE), lambda b, g: (b, 0, coff + g)),
        pl.BlockSpec((None, None, DT_ROWS, L), lambda b, g: (b, g, 0, 0)),
        pl.BlockSpec((CONV_K, gw), lambda b, g: (0, g)),
        pl.BlockSpec((CONV_K, D_STATE), lambda b, g: (0, cwb_off + g)),
        pl.BlockSpec((CONV_K, D_STATE), lambda b, g: (0, cwb_off + G + g)),
        pl.BlockSpec((1, gw), lambda b, g: (0, g)),
        pl.BlockSpec((1, D_STATE), lambda b, g: (0, cwb_off + g)),
        pl.BlockSpec((1, D_STATE), lambda b, g: (0, cwb_off + G + g)),
        pl.BlockSpec((None, DT_ROWS, 1), lambda b, g: (g, 0, 0)),
        pl.BlockSpec((None, DT_ROWS, 1), lambda b, g: (g, 0, 0)),
        pl.BlockSpec((None, 1, LANES), lambda b, g: (g, 0, 0)),
        pl.BlockSpec((1, gw), lambda b, g: (0, g)),
        pl.BlockSpec((1, gw), lambda b, g: (0, g)),
    ]
    scratch = [
        pltpu.VMEM((L, gw), BF16),
        pltpu.VMEM((L, D_STATE), BF16),
        pltpu.VMEM((D_STATE, L), BF16),
        pltpu.VMEM((L, CHUNK), BF16),
        pltpu.VMEM((L, LANES), F32),
        pltpu.VMEM((DT_ROWS, L), F32),
        pltpu.VMEM((L // 2, gw), BF16),
        pltpu.VMEM((D_STATE, gw), F32),
        pltpu.VMEM((D_STATE, gw), F32),
    ]
    return pl.pallas_call(
        functools.partial(_ssd_kernel, L=L, P=P),
        grid=(B, G),
        in_specs=in_specs,
        out_specs=late((None, L, gw), lambda b, g: (b, 0, g)),
        out_shape=jax.ShapeDtypeStruct((B, L, DI), BF16),
        scratch_shapes=scratch,
        compiler_params=_cparams("parallel", "parallel"),
        name="ssd",
    )(proj3, proj3, proj3, proj3, dt4, conv_w, conv_w, conv_w, conv_b, conv_b, conv_b,
      dt_bias_c, a_log_c, a_log_l, d_skip_x, gnorm_w)


def _fft1_kernel(w_ref, tw_ref, u_ref, o_ref, *, L1, nl2, C):
    res = jnp.dot(w_ref[...], u_ref[...], preferred_element_type=F32)
    rep = C // LANES
    for q in range(nl2):
        a = res[:L1, q * C:(q + 1) * C]
        b = res[L1:, q * C:(q + 1) * C]
        twr = jnp.concatenate([tw_ref[0, q]] * rep, axis=1)
        twi = jnp.concatenate([tw_ref[1, q]] * rep, axis=1)
        o_ref[0, q] = (a * twr - b * twi).astype(o_ref.dtype)
        o_ref[1, q] = (a * twi + b * twr).astype(o_ref.dtype)


def _fft_stage1(u3, L1):
    B, L, C = u3.shape
    L2 = L // L1
    nl2 = 8
    k1 = np.arange(L1)
    ang1 = 2.0 * np.pi * np.outer(k1, np.arange(L1)) / L1
    w1 = jnp.asarray(np.concatenate([np.cos(ang1), -np.sin(ang1)], axis=0), F32).astype(BF16)
    angt = 2.0 * np.pi * np.outer(np.arange(L2), k1) / L
    tw = np.stack([np.cos(angt), -np.sin(angt)], axis=0)[..., None]
    tw = jnp.asarray(np.broadcast_to(tw, (2, L2, L1, LANES)), F32)
    uv = u3.reshape(B, L1, L2 * C)
    return pl.pallas_call(
        functools.partial(_fft1_kernel, L1=L1, nl2=nl2, C=C),
        grid=(B, L2 // nl2),
        in_specs=[pl.BlockSpec((2 * L1, L1), lambda b, j: (0, 0)),
                  pl.BlockSpec((2, nl2, L1, LANES), lambda b, j: (0, j, 0, 0)),
                  pl.BlockSpec((None, L1, nl2 * C), lambda b, j: (b, 0, j))],
        out_specs=pl.BlockSpec((None, 2, nl2, L1, C), lambda b, j: (b, 0, j, 0, 0)),
        out_shape=jax.ShapeDtypeStruct((B, 2, L2, L1, C), BF16),
        compiler_params=_cparams("parallel", "parallel"),
        name="fft_stage1",
    )(w1, tw, uv)


def _fft_stage2(g5):
    B, _, L2, L1, C = g5.shape
    ang = 2.0 * np.pi * np.outer(np.arange(L2), np.arange(L2)) / L2
    cs, sn = np.cos(ang), np.sin(ang)
    w2 = jnp.asarray(np.block([[cs, sn], [-sn, cs]]), F32).astype(BF16)
    N = L1 * C
    tn = _tile(N, 4096, LANES)
    gv = g5.reshape(B, 2 * L2, N)
    y = pl.pallas_call(
        _mm_kernel,
        grid=(B, N // tn),
        in_specs=[pl.BlockSpec((2 * L2, 2 * L2), lambda b, j: (0, 0)),
                  pl.BlockSpec((None, 2 * L2, tn), lambda b, j: (b, 0, j))],
        out_specs=pl.BlockSpec((None, 2 * L2, tn), lambda b, j: (b, 0, j)),
        out_shape=jax.ShapeDtypeStruct((B, 2 * L2, N), BF16),
        compiler_params=_cparams("parallel", "parallel"),
        name="fft_stage2",
    )(w2, gv)
    return y.reshape(B, 2, L2 * L1, C)


def _four_out_kernel(y_ref, cs_ref, wf_ref, g_ref, t_ref, o_ref, f_s, *, scale):
    @pl.when(pl.program_id(2) == 0)
    def _():
        yr = y_ref[0]
        yi = y_ref[1]
        cs = cs_ref[...]
        parts = []
        for g in range(FOUR_GROUPS):
            sl = slice(g * FOUR_GROUP_DIM, (g + 1) * FOUR_GROUP_DIM)
            lhs = jnp.concatenate([yr[:, sl], yi[:, sl]], axis=1)
            parts.append(jnp.dot(lhs, cs, preferred_element_type=F32))
        f_s[...] = (jnp.concatenate(parts, axis=1) * scale).astype(f_s.dtype)

    yf = jnp.dot(f_s[...], wf_ref[...], preferred_element_type=F32)
    o_ref[...] = (t_ref[...].astype(F32) + _sigmoid(g_ref[...].astype(F32)) * yf).astype(o_ref.dtype)


def _four_out(y4, w_four, proj, gate_off, t1):
    B, _, L, C = y4.shape
    D = w_four.shape[1]
    tm = _tile(L, 1024)
    tn = _tile(D, 2048, LANES, divides=(gate_off,))
    goff = gate_off // tn
    nti = L // tm
    ang = 2.0 * np.pi * np.outer(np.arange(FOUR_GROUP_DIM), np.arange(FOUR_GROUP_DIM)) / FOUR_GROUP_DIM
    cs = jnp.asarray(np.concatenate([np.cos(ang), np.sin(ang)], axis=0), F32).astype(BF16)
    scale = 1.0 / math.sqrt(L * FOUR_GROUP_DIM)
    return pl.pallas_call(
        functools.partial(_four_out_kernel, scale=scale),
        grid=(B, nti, D // tn),
        in_specs=[pl.BlockSpec((None, 2, tm, C), lambda b, i, j: (b, 0, i, 0)),
                  pl.BlockSpec((2 * FOUR_GROUP_DIM, FOUR_GROUP_DIM), lambda b, i, j: (0, 0)),
                  pl.BlockSpec((C, tn), lambda b, i, j: (0, j)),
                  pl.BlockSpec((tm, tn), lambda b, i, j: (b * nti + i, goff + j)),
                  pl.BlockSpec((tm, tn), lambda b, i, j: (b * nti + i, j))],
        out_specs=pl.BlockSpec((tm, tn), lambda b, i, j: (b * nti + i, j)),
        out_shape=jax.ShapeDtypeStruct((B * L, D), BF16),
        scratch_shapes=[pltpu.VMEM((tm, C), BF16)],
        compiler_params=_cparams("parallel", "parallel", "arbitrary"),
        name="four_out",
    )(y4, cs, w_four, proj, t1)


def _chan_dft_kernel(u_ref, cs_ref, o_ref):
    u = u_ref[...]
    cs = cs_ref[...]
    cos_parts, sin_parts = [], []
    for g in range(FOUR_GROUPS):
        r = jnp.dot(u[:, g * FOUR_GROUP_DIM:(g + 1) * FOUR_GROUP_DIM], cs, preferred_element_type=F32)
        cos_parts.append(r[:, :FOUR_GROUP_DIM])
        sin_parts.append(r[:, FOUR_GROUP_DIM:])
    o_ref[0] = jnp.concatenate(cos_parts, axis=1).astype(o_ref.dtype)
    o_ref[1] = jnp.concatenate(sin_parts, axis=1).astype(o_ref.dtype)


def _chan_dft(u, B, L):
    T, C = u.shape
    tm = _tile(L, 1024)
    nti = L // tm
    ang = 2.0 * np.pi * np.outer(np.arange(FOUR_GROUP_DIM), np.arange(FOUR_GROUP_DIM)) / FOUR_GROUP_DIM
    cs = jnp.asarray(np.concatenate([np.cos(ang), np.sin(ang)], axis=1), F32).astype(BF16)
    return pl.pallas_call(
        _chan_dft_kernel,
        grid=(B, nti),
        in_specs=[pl.BlockSpec((tm, C), lambda b, i: (b * nti + i, 0)),
                  pl.BlockSpec((FOUR_GROUP_DIM, 2 * FOUR_GROUP_DIM), lambda b, i: (0, 0))],
        out_specs=pl.BlockSpec((None, 2, tm, C), lambda b, i: (b, 0, i, 0)),
        out_shape=jax.ShapeDtypeStruct((B, 2, L, C), BF16),
        compiler_params=_cparams("parallel", "parallel"),
        name="chan_dft",
    )(u, cs)


def _dense_four_kernel(d_ref, z_ref, wf_ref, g_ref, t_ref, o_ref, acc_s, *, scale):
    k = pl.program_id(2)

    @pl.when(k == 0)
    def _():
        acc_s[...] = jnp.zeros_like(acc_s)

    acc_s[...] += jnp.dot(d_ref[...], z_ref[...], preferred_element_type=F32)

    @pl.when(k == pl.num_programs(2) - 1)
    def _():
        f = (acc_s[...] * scale).astype(BF16)
        yf = jnp.dot(f, wf_ref[...], preferred_element_type=F32)
        o_ref[...] = (t_ref[...].astype(F32) + _sigmoid(g_ref[...].astype(F32)) * yf).astype(o_ref.dtype)


def _dense_four_out(z4, w_four, proj, gate_off, t1):
    B, _, L, C = z4.shape
    D = w_four.shape[1]
    tm = _tile(L, 512)
    tk = _tile(2 * L, 2048, LANES)
    nti = L // tm
    assert gate_off % D == 0
    goff = gate_off // D
    prod = (jnp.arange(L, dtype=jnp.int32)[:, None] * jnp.arange(L, dtype=jnp.int32)[None, :]) % L
    ang = prod.astype(F32) * (2.0 * math.pi / L)
    dmat = jnp.concatenate([jnp.cos(ang), -jnp.sin(ang)], axis=1).astype(BF16)
    scale = 1.0 / math.sqrt(L * FOUR_GROUP_DIM)
    return pl.pallas_call(
        functools.partial(_dense_four_kernel, scale=scale),
        grid=(B, nti, (2 * L) // tk),
        in_specs=[pl.BlockSpec((tm, tk), lambda b, i, k: (i, k)),
                  pl.BlockSpec((None, tk, C), lambda b, i, k: (b, k, 0)),
                  pl.BlockSpec((C, D), lambda b, i, k: (0, 0)),
                  pl.BlockSpec((tm, D), lambda b, i, k: (b * nti + i, goff)),
                  pl.BlockSpec((tm, D), lambda b, i, k: (b * nti + i, 0))],
        out_specs=pl.BlockSpec((tm, D), lambda b, i, k: (b * nti + i, 0)),
        out_shape=jax.ShapeDtypeStruct((B * L, D), BF16),
        scratch_shapes=[pltpu.VMEM((tm, C), F32)],
        compiler_params=_cparams("parallel", "parallel", "arbitrary"),
        name="dense_four_out",
    )(dmat, z4.reshape(B, 2 * L, C), w_four, proj, t1)


def _ffn_kernel(te_ref, act_ref, x_ref, w1_ref, w3_ref, w2_ref, o_ref, acc_s):
    i = pl.program_id(0)
    j = pl.program_id(1)
    nj = pl.num_programs(1)

    @pl.when(j == 0)
    def _():
        acc_s[...] = jnp.zeros_like(acc_s)

    @pl.when(act_ref[i] > 0)
    def _():
        x = x_ref[...]
        h1 = jnp.dot(x, w1_ref[...], preferred_element_type=F32)
        h3 = jnp.dot(x, w3_ref[...], preferred_element_type=F32)
        h = (_silu(h1) * h3).astype(BF16)
        acc_s[...] += jnp.dot(h, w2_ref[...], preferred_element_type=F32)

    @pl.when(j == nj - 1)
    def _():
        o_ref[...] = acc_s[...].astype(o_ref.dtype)


def _ffn(x, w1, w3, w2, tile_expert, tile_active, tm):
    R, D = x.shape
    E, _, F = w1.shape
    tf = _tile(F, 512, LANES)
    nf = F // tf

    def wcol(i, j, te, act):
        return (te[i], 0, jnp.where(act[i] > 0, j, nf - 1))

    def wrow(i, j, te, act):
        return (te[i], jnp.where(act[i] > 0, j, nf - 1), 0)

    grid_spec = pltpu.PrefetchScalarGridSpec(
        num_scalar_prefetch=2,
        grid=(R // tm, nf),
        in_specs=[pl.BlockSpec((tm, D), lambda i, j, te, act: (i, 0)),
                  pl.BlockSpec((None, D, tf), wcol),
                  pl.BlockSpec((None, D, tf), wcol),
                  pl.BlockSpec((None, tf, D), wrow)],
        out_specs=pl.BlockSpec((tm, D), lambda i, j, te, act: (i, 0)),
        scratch_shapes=[pltpu.VMEM((tm, D), F32)],
    )
    return pl.pallas_call(
        _ffn_kernel,
        grid_spec=grid_spec,
        out_shape=jax.ShapeDtypeStruct((R, D), BF16),
        compiler_params=_cparams("parallel", "arbitrary"),
        name="ffn",
    )(tile_expert, tile_active, x, w1, w3, w2)


def _router_kernel(x_ref, r_ref, w_ref, i_ref, *, E):
    xh, xm, _ = _split3(x_ref[...])
    rh, rm, _ = _split3(r_ref[...])
    logits = (jnp.dot(xh, rh, preferred_element_type=F32) + jnp.dot(xm, rh, preferred_element_type=F32)
              + jnp.dot(xh, rm, preferred_element_type=F32))
    lane = lax.broadcasted_iota(jnp.int32, logits.shape, 1).astype(F32)
    logits = jnp.where(lane < E, logits, -jnp.inf)
    ex = jnp.exp(logits - jnp.max(logits, axis=-1, keepdims=True))
    probs = ex / jnp.sum(ex, axis=-1, keepdims=True)
    m1 = jnp.max(probs, axis=-1, keepdims=True)
    i1 = jnp.min(jnp.where(probs == m1, lane, float(LANES)), axis=-1, keepdims=True)
    rest = jnp.where(lane == i1, -1.0, probs)
    m2 = jnp.max(rest, axis=-1, keepdims=True)
    i2 = jnp.min(jnp.where(rest == m2, lane, float(LANES)), axis=-1, keepdims=True)
    den = m1 + m2
    w_ref[...] = jnp.where(lane == 0.0, m1 / den, jnp.where(lane == 1.0, m2 / den, 0.0))
    i_ref[...] = jnp.where(lane == 0.0, i1, jnp.where(lane == 1.0, i2, 0.0)).astype(jnp.int32)


def _router(x, router):
    T, D = x.shape
    E = router.shape[1]
    tm = _tile(T, 512)
    rp = jnp.zeros((D, LANES), F32).at[:, :E].set(router)
    row = pl.BlockSpec((tm, LANES), lambda i: (i, 0))
    return pl.pallas_call(
        functools.partial(_router_kernel, E=E),
        grid=(T // tm,),
        in_specs=[pl.BlockSpec((tm, D), lambda i: (i, 0)), pl.BlockSpec((D, LANES), lambda i: (0, 0))],
        out_specs=[row, row],
        out_shape=[jax.ShapeDtypeStruct((T, LANES), F32), jax.ShapeDtypeStruct((T, LANES), jnp.int32)],
        compiler_params=_cparams("parallel"),
        name="router",
    )(x, rp)


def _moe(xf, xb, router, we1, we3, we2, ln_g, ln_b):
    T, D = xf.shape
    E = we1.shape[0]
    tm = _tile(T, 1024)
    wt, idx = _router(xf, router)
    flat_e = idx[:, :TOP_K].reshape(-1)
    onehot = (flat_e[:, None] == jnp.arange(E, dtype=jnp.int32)[None, :]).astype(jnp.int32)
    rank = jnp.sum((jnp.cumsum(onehot, axis=0) - onehot) * onehot, axis=1)
    counts = jnp.sum(onehot, axis=0)
    padded = ((counts + tm - 1) // tm) * tm
    ends = jnp.cumsum(padded)
    pos = (ends - padded)[flat_e] + rank
    n_tiles = (TOP_K * T) // tm + E
    src = jnp.zeros((n_tiles * tm,), jnp.int32).at[pos].set(
        jnp.arange(TOP_K * T, dtype=jnp.int32) // TOP_K, unique_indices=True, mode="promise_in_bounds")
    starts = jnp.arange(n_tiles, dtype=jnp.int32) * tm
    tile_expert = jnp.minimum(jnp.searchsorted(ends, starts, side="right"), E - 1).astype(jnp.int32)
    tile_active = (starts < ends[-1]).astype(jnp.int32)
    x_sorted = xb.at[src].get(mode="promise_in_bounds")
    y_sorted = _ffn(x_sorted, we1, we3, we2, tile_expert, tile_active, tm)
    pos2 = pos.reshape(T, TOP_K)
    ya = y_sorted.at[pos2[:, 0]].get(mode="promise_in_bounds", unique_indices=True)
    yb = y_sorted.at[pos2[:, 1]].get(mode="promise_in_bounds", unique_indices=True)
    return _combine_ln(xf, ya, yb, wt, ln_g, ln_b)


def _prep_mixer(w_in, conv_w, conv_b, dt_bias_f, dt_bias_b, a_log_f, a_log_b, d_skip, gnorm_w,
                w_ssd_up, w_four, w_o):
    D = w_in.shape[0]
    DI = w_ssd_up.shape[0]
    CD = conv_b.shape[0]
    H = dt_bias_f.shape[0]
    DF = w_four.shape[0]
    G = SSD_GROUPS
    hpg = H // G
    o_dtf = DI + CD
    o_dtb = o_dtf + H
    o_u = o_dtb + H
    o_g = o_u + DF
    w_main = jnp.concatenate([w_in[:, :o_dtf], w_in[:, o_g:]], axis=1).astype(BF16)
    w_u = w_in[:, o_u:o_g].astype(BF16)

    def dt_rows(f, b, fill=0.0):
        lead = f.shape[:-1]
        out = jnp.full(lead + (G, DT_ROWS), fill, F32)
        out = out.at[..., :hpg].set(f.reshape(lead + (G, hpg)))
        return out.at[..., BWD_ROW0:BWD_ROW0 + hpg].set(b.reshape(lead + (G, hpg)))

    w_dt = dt_rows(w_in[:, o_dtf:o_dtb], w_in[:, o_dtb:o_u]).reshape(D, G * DT_ROWS).astype(BF16)
    dt_bias = dt_rows(dt_bias_f, dt_bias_b)
    a_log = dt_rows(a_log_f, a_log_b)
    a_log_l = jnp.zeros((G, 1, LANES), F32).at[:, 0, :DT_ROWS].set(a_log)
    return dict(
        w_main=w_main, w_udt=jnp.concatenate([w_u, w_dt], axis=1), w_u_cols=DF,
        conv_w=conv_w.reshape(CONV_K, CD), conv_b=conv_b.reshape(1, CD),
        dt_bias_c=dt_bias[..., None], a_log_c=a_log[..., None], a_log_l=a_log_l,
        d_skip_x=jnp.repeat(d_skip, DI // H).reshape(1, DI), gnorm_w=gnorm_w.reshape(1, DI),
        w_ssd_up=w_ssd_up.astype(BF16), w_four=w_four.astype(BF16), w_o=w_o.astype(BF16),
        DI=DI, P=DI // H, gate_off=o_dtf, D=D)


def _token_mixer(xf, xb, B, L, mp, ln_g, ln_b):
    T = B * L
    G = SSD_GROUPS
    DI, D = mp["DI"], mp["D"]
    proj = _mm(xb, mp["w_main"], BF16)
    u, dt = _mm_split(xb, mp["w_udt"], mp["w_u_cols"], BF16, F32)
    dt4 = dt.reshape(B, L, G, DT_ROWS).transpose(0, 2, 3, 1)
    v = _ssd(proj.reshape(B, L, -1), dt4, mp["conv_w"], mp["conv_b"], mp["dt_bias_c"], mp["a_log_c"],
             mp["a_log_l"], mp["d_skip_x"], mp["gnorm_w"], DI, mp["P"])
    t1 = _gate_mm(v.reshape(T, DI), mp["w_ssd_up"], proj, mp["gate_off"])
    if L <= DENSE_DFT_MAX_L:
        merged = _dense_four_out(_chan_dft(u, B, L), mp["w_four"], proj, mp["gate_off"] + D, t1)
    else:
        g5 = _fft_stage1(u.reshape(B, L, -1), L // FFT_L2)
        y4 = _fft_stage2(g5)
        merged = _four_out(y4, mp["w_four"], proj, mp["gate_off"] + D, t1)
    return _mm_res_ln(merged, mp["w_o"], xf, ln_g, ln_b)


def kernel(x_prompt, x_sample, ln_in_g, ln_in_b, w_in_0, conv_w_0, conv_b_0, dt_bias_f_0, dt_bias_b_0, a_log_f_0, a_log_b_0, d_skip_0, gnorm_w_0, w_ssd_up_0, w_four_0, w_o_0, ln1_g_0, ln1_b_0, w1_0, w3_0, w2_0, ln2_g_0, ln2_b_0, w_in_1, conv_w_1, conv_b_1, dt_bias_f_1, dt_bias_b_1, a_log_f_1, a_log_b_1, d_skip_1, gnorm_w_1, w_ssd_up_1, w_four_1, w_o_1, ln1_g_1, ln1_b_1, router_1, we1_1, we3_1, we2_1, ln2_g_1, ln2_b_1):
    mp0 = _prep_mixer(w_in_0, conv_w_0, conv_b_0, dt_bias_f_0, dt_bias_b_0, a_log_f_0, a_log_b_0,
                      d_skip_0, gnorm_w_0, w_ssd_up_0, w_four_0, w_o_0)
    mp1 = _prep_mixer(w_in_1, conv_w_1, conv_b_1, dt_bias_f_1, dt_bias_b_1, a_log_f_1, a_log_b_1,
                      d_skip_1, gnorm_w_1, w_ssd_up_1, w_four_1, w_o_1)
    w1 = w1_0.astype(BF16)[None]
    w3 = w3_0.astype(BF16)[None]
    w2 = w2_0.astype(BF16)[None]
    we1, we3, we2, x_prompt = lax.optimization_barrier(
        (we1_1.astype(BF16), we3_1.astype(BF16), we2_1.astype(BF16), x_prompt))

    def trunk(x):
        B, L, D = x.shape
        T = B * L
        xf, xb = _ln_in(x.reshape(T, D), ln_in_g, ln_in_b)
        xf, xb = _token_mixer(xf, xb, B, L, mp0, ln1_g_0, ln1_b_0)
        tm = _tile(T, 1024)
        n_tiles = T // tm
        ffn = _ffn(xb, w1, w3, w2, jnp.zeros((n_tiles,), jnp.int32), jnp.ones((n_tiles,), jnp.int32), tm)
        xf, xb = _res_ln(xf, ffn, ln2_g_0, ln2_b_0)
        xf, xb = _token_mixer(xf, xb, B, L, mp1, ln1_g_1, ln1_b_1)
        xf, _ = _moe(xf, xb, router_1, we1, we3, we2, ln2_g_1, ln2_b_1)
        return xf.reshape(B, L, D)

    return (trunk(x_prompt), trunk(x_sample))
```

```python
import functools
import math

import numpy as np
import jax
import jax.numpy as jnp
from jax import lax
from jax.experimental import pallas as pl
from jax.experimental.pallas import tpu as pltpu

F32 = jnp.float32
BF16 = jnp.bfloat16

SSD_GROUPS = 8
D_STATE = 128
CONV_K = 5
CHUNK = 128
FOUR_GROUPS = 8
FOUR_GROUP_DIM = 128
TOP_K = 2
DEPTH = 2
ALPHA = (2 * DEPTH) ** 0.25
LN_EPS = 1e-5
RMS_EPS = 1e-5
LOG2E = math.log2(math.e)

LANES = 128
BF16_SUBLANES = 16
VMEM_LIMIT_BYTES = 58 * 1024 * 1024
MM_ROWS = 1024
MM_COLS = 1024
LN_ROWS = 512
FFN_COLS = 512
FOUR_COLS = 2048
DFT_ROWS = 512
FFT2_COLS = 4096

DT_ROWS = 16
BWD_ROW0 = 8
FFT_L2 = 128
DENSE_DFT_MAX_L = 2048


def _cparams(*sem):
    return pltpu.CompilerParams(dimension_semantics=sem, vmem_limit_bytes=VMEM_LIMIT_BYTES)


def _tile(n, pref, align=8, divides=()):
    t = min(pref, n)
    t -= t % align
    while t > align:
        if n % t == 0 and all(d % t == 0 for d in divides):
            return t
        t -= align
    return align


def _sigmoid(x):
    return 0.5 + 0.5 * jnp.tanh(0.5 * x)


def _silu(x):
    hx = 0.5 * x
    return hx + hx * jnp.tanh(hx)


def _ln_rows(v, g, b):
    mu = jnp.mean(v, axis=-1, keepdims=True)
    d = v - mu
    var = jnp.mean(d * d, axis=-1, keepdims=True)
    return d * lax.rsqrt(var + LN_EPS) * g + b


def _split3(x):
    hi = x.astype(BF16)
    r1 = x - hi.astype(F32)
    mid = r1.astype(BF16)
    lo = (r1 - mid.astype(F32)).astype(BF16)
    return hi, mid, lo


def _ln_in_kernel(x_ref, g_ref, b_ref, xf_ref, xb_ref):
    y = _ln_rows(x_ref[...], g_ref[...], b_ref[...])
    xf_ref[...] = y
    xb_ref[...] = y.astype(BF16)


def _ln_in(x, g, b):
    T, D = x.shape
    tm = _tile(T, LN_ROWS)
    row = pl.BlockSpec((tm, D), lambda i: (i, 0))
    vec = pl.BlockSpec((1, D), lambda i: (0, 0))
    return pl.pallas_call(
        _ln_in_kernel,
        grid=(T // tm,),
        in_specs=[row, vec, vec],
        out_specs=[row, row],
        out_shape=[jax.ShapeDtypeStruct((T, D), F32), jax.ShapeDtypeStruct((T, D), BF16)],
        compiler_params=_cparams("parallel"),
        name="ln_in",
    )(x, g.reshape(1, D), b.reshape(1, D))


def _res_ln_kernel(x_ref, y_ref, g_ref, b_ref, xf_ref, xb_ref):
    v = ALPHA * x_ref[...] + y_ref[...].astype(F32)
    y = _ln_rows(v, g_ref[...], b_ref[...])
    xf_ref[...] = y
    xb_ref[...] = y.astype(BF16)


def _res_ln(x, y, g, b):
    T, D = x.shape
    tm = _tile(T, LN_ROWS)
    row = pl.BlockSpec((tm, D), lambda i: (i, 0))
    vec = pl.BlockSpec((1, D), lambda i: (0, 0))
    return pl.pallas_call(
        _res_ln_kernel,
        grid=(T // tm,),
        in_specs=[row, row, vec, vec],
        out_specs=[row, row],
        out_shape=[jax.ShapeDtypeStruct((T, D), F32), jax.ShapeDtypeStruct((T, D), BF16)],
        compiler_params=_cparams("parallel"),
        name="res_ln",
    )(x, y, g.reshape(1, D), b.reshape(1, D))


def _combine_ln_kernel(x_ref, ya_ref, yb_ref, w_ref, g_ref, b_ref, xf_ref, xb_ref):
    w = w_ref[...]
    mix = w[:, 0:1] * ya_ref[...].astype(F32) + w[:, 1:2] * yb_ref[...].astype(F32)
    y = _ln_rows(ALPHA * x_ref[...] + mix, g_ref[...], b_ref[...])
    xf_ref[...] = y
    xb_ref[...] = y.astype(BF16)


def _combine_ln(x, ya, yb, w, g, b):
    T, D = x.shape
    tm = _tile(T, LN_ROWS)
    row = pl.BlockSpec((tm, D), lambda i: (i, 0))
    wsp = pl.BlockSpec((tm, LANES), lambda i: (i, 0))
    vec = pl.BlockSpec((1, D), lambda i: (0, 0))
    return pl.pallas_call(
        _combine_ln_kernel,
        grid=(T // tm,),
        in_specs=[row, row, row, wsp, vec, vec],
        out_specs=[row, row],
        out_shape=[jax.ShapeDtypeStruct((T, D), F32), jax.ShapeDtypeStruct((T, D), BF16)],
        compiler_params=_cparams("parallel"),
        name="combine_ln",
    )(x, ya, yb, w, g.reshape(1, D), b.reshape(1, D))


def _mm_kernel(a_ref, w_ref, o_ref):
    o_ref[...] = jnp.dot(a_ref[...], w_ref[...], preferred_element_type=F32).astype(o_ref.dtype)


def _mm(a, w, out_dtype):
    M, K = a.shape
    N = w.shape[1]
    tm = _tile(M, MM_ROWS)
    tn = _tile(N, MM_COLS, LANES)
    return pl.pallas_call(
        _mm_kernel,
        grid=(M // tm, N // tn),
        in_specs=[pl.BlockSpec((tm, K), lambda i, j: (i, 0)), pl.BlockSpec((K, tn), lambda i, j: (0, j))],
        out_specs=pl.BlockSpec((tm, tn), lambda i, j: (i, j)),
        out_shape=jax.ShapeDtypeStruct((M, N), out_dtype),
        compiler_params=_cparams("parallel", "arbitrary"),
        name="mm",
    )(a, w)


def _mm_split_kernel(a_ref, w_ref, o1_ref, o2_ref):
    r = jnp.dot(a_ref[...], w_ref[...], preferred_element_type=F32)
    n1 = o1_ref.shape[1]
    o1_ref[...] = r[:, :n1].astype(o1_ref.dtype)
    o2_ref[...] = r[:, n1:].astype(o2_ref.dtype)


def _mm_split(a, w, n1, dtype1, dtype2):
    M, K = a.shape
    N = w.shape[1]
    tm = _tile(M, MM_ROWS)
    return pl.pallas_call(
        _mm_split_kernel,
        grid=(M // tm,),
        in_specs=[pl.BlockSpec((tm, K), lambda i: (i, 0)), pl.BlockSpec((K, N), lambda i: (0, 0))],
        out_specs=[pl.BlockSpec((tm, n1), lambda i: (i, 0)), pl.BlockSpec((tm, N - n1), lambda i: (i, 0))],
        out_shape=[jax.ShapeDtypeStruct((M, n1), dtype1), jax.ShapeDtypeStruct((M, N - n1), dtype2)],
        compiler_params=_cparams("parallel"),
        name="mm_split",
    )(a, w)


def _gate_mm_kernel(a_ref, w_ref, g_ref, o_ref):
    y = jnp.dot(a_ref[...], w_ref[...], preferred_element_type=F32)
    o_ref[...] = (_sigmoid(g_ref[...].astype(F32)) * y).astype(o_ref.dtype)


def _gate_mm(a, w, proj, gate_off):
    M, K = a.shape
    N = w.shape[1]
    tm = _tile(M, MM_ROWS)
    tn = _tile(N, MM_COLS, LANES, divides=(gate_off,))
    goff = gate_off // tn
    return pl.pallas_call(
        _gate_mm_kernel,
        grid=(M // tm, N // tn),
        in_specs=[pl.BlockSpec((tm, K), lambda i, j: (i, 0)),
                  pl.BlockSpec((K, tn), lambda i, j: (0, j)),
                  pl.BlockSpec((tm, tn), lambda i, j: (i, goff + j))],
        out_specs=pl.BlockSpec((tm, tn), lambda i, j: (i, j)),
        out_shape=jax.ShapeDtypeStruct((M, N), BF16),
        compiler_params=_cparams("parallel", "arbitrary"),
        name="gate_mm",
    )(a, w, proj)


def _mm_res_ln_kernel(a_ref, w_ref, x_ref, g_ref, b_ref, xf_ref, xb_ref):
    y = jnp.dot(a_ref[...], w_ref[...], preferred_element_type=F32)
    out = _ln_rows(ALPHA * x_ref[...] + y, g_ref[...], b_ref[...])
    xf_ref[...] = out
    xb_ref[...] = out.astype(BF16)


def _mm_res_ln(a, w, x, g, b):
    M, K = a.shape
    D = w.shape[1]
    tm = _tile(M, LN_ROWS)
    row = pl.BlockSpec((tm, D), lambda i: (i, 0))
    vec = pl.BlockSpec((1, D), lambda i: (0, 0))
    return pl.pallas_call(
        _mm_res_ln_kernel,
        grid=(M // tm,),
        in_specs=[pl.BlockSpec((tm, K), lambda i: (i, 0)), pl.BlockSpec((K, D), lambda i: (0, 0)), row, vec, vec],
        out_specs=[row, row],
        out_shape=[jax.ShapeDtypeStruct((M, D), F32), jax.ShapeDtypeStruct((M, D), BF16)],
        compiler_params=_cparams("parallel"),
        name="mm_res_ln",
    )(a, w, x, g.reshape(1, D), b.reshape(1, D))


PACK_DT = 0
PACK_STATE = DT_ROWS
PACK_OFF = 2 * DT_ROWS
PACK_E = 3 * DT_ROWS


def _conv_silu_chunk(src_ref, w, bias, c, nc, L):
    halo = BF16_SUBLANES
    n_ext = CHUNK + 2 * halo
    W = src_ref.shape[1]
    s0 = pl.multiple_of(c * CHUNK, CHUNK)
    sp = pl.multiple_of(jnp.maximum(s0 - halo, 0), halo)
    sn = pl.multiple_of(jnp.minimum(s0 + CHUNK, L - halo), halo)
    cur = src_ref[pl.ds(s0, CHUNK), :].astype(F32)
    prev = jnp.where(c > 0, src_ref[pl.ds(sp, halo), :].astype(F32), 0.0)
    nxt = jnp.where(c < nc - 1, src_ref[pl.ds(sn, halo), :].astype(F32), 0.0)
    ext = jnp.concatenate([prev, cur, nxt], axis=0)
    acc = jnp.broadcast_to(bias, (CHUNK, W))
    for k in range(CONV_K):
        shift = (n_ext - halo - (k - CONV_K // 2)) % n_ext
        acc = acc + w[k:k + 1, :] * pltpu.roll(ext, shift, 0)[:CHUNK]
    return _silu(acc)


def _ssd_kernel(z_ref, xs_ref, b_ref, c_ref, dt_ref, cwx_ref, cwb_ref, cwc_ref, cbx_ref, cbb_ref, cbc_ref,
                dtb_ref, alc_ref, all_ref, dsk_ref, gnw_ref, o_ref,
                xs_s, c_s, bt_s, cb_s, pack_s, erow_s, yb_s, stf_s, stb_s, *, L, P):
    nc = L // CHUNK
    half = nc // 2
    gw = xs_s.shape[1]

    ri = lax.broadcasted_iota(jnp.int32, (CHUNK, CHUNK), 0)
    ci = lax.broadcasted_iota(jnp.int32, (CHUNK, CHUNK), 1)
    incl_upper = (ri <= ci).astype(BF16)
    row16 = lax.broadcasted_iota(jnp.int32, (DT_ROWS, CHUNK), 0)
    zpad = jnp.zeros((CHUNK - DT_ROWS, CHUNK), F32)
    a_c = -jnp.exp(alc_ref[...])
    a_l = -jnp.exp(all_ref[...])
    dt_bias = dtb_ref[...]
    wx, wb, wc = cwx_ref[...], cwb_ref[...], cwc_ref[...]
    bx, bb, bc_bias = cbx_ref[...], cbb_ref[...], cbc_ref[...]

    def prepass(c, carry):
        s0 = pl.multiple_of(c * CHUNK, CHUNK)
        xs_s[pl.ds(s0, CHUNK), :] = _conv_silu_chunk(xs_ref, wx, bx, c, nc, L).astype(BF16)
        bcv = _conv_silu_chunk(b_ref, wb, bb, c, nc, L).astype(BF16)
        ccv = _conv_silu_chunk(c_ref, wc, bc_bias, c, nc, L).astype(BF16)
        c_s[pl.ds(s0, CHUNK), :] = ccv
        bt_s[:, pl.ds(s0, CHUNK)] = bcv.astype(F32).T.astype(BF16)
        cb = lax.dot_general(ccv, bcv, (((1,), (1,)), ((), ())), preferred_element_type=F32)
        cb_s[pl.ds(s0, CHUNK), :] = cb.astype(BF16)
        raw = dt_ref[:, pl.ds(s0, CHUNK)] + dt_bias
        dtv = jnp.maximum(raw, 0.0) + jnp.log1p(jnp.exp(-jnp.abs(raw)))
        a_row = dtv * a_c
        hi, mid, lo = _split3(a_row)
        acum_row = (jnp.dot(hi, incl_upper, preferred_element_type=F32)
                    + jnp.dot(mid, incl_upper, preferred_element_type=F32)
                    + jnp.dot(lo, incl_upper, preferred_element_type=F32))
        erow_s[:, pl.ds(s0, CHUNK)] = LOG2E * jnp.where(row16 < BWD_ROW0, acum_row, acum_row - a_row)
        acum_col = jnp.concatenate([acum_row, zpad], axis=0).T
        dt_col = jnp.concatenate([dtv, zpad], axis=0).T
        total = acum_col[CHUNK - 1:CHUNK, :]
        fwd_lane = ci < BWD_ROW0
        e_bwd = acum_col - dt_col * a_l
        e_col = jnp.where(fwd_lane, acum_col, e_bwd)
        f_state = dt_col * jnp.exp(jnp.where(fwd_lane, total - acum_col, e_bwd))
        f_off = jnp.exp(jnp.where(fwd_lane, acum_col, total - e_bwd))
        pack = jnp.where(ci < PACK_STATE, dt_col,
                         jnp.where(ci < PACK_OFF, pltpu.roll(f_state, PACK_STATE, 1),
                                   jnp.where(ci < PACK_E, pltpu.roll(f_off, PACK_OFF, 1),
                                             jnp.where(ci < PACK_E + DT_ROWS, pltpu.roll(LOG2E * e_col, PACK_E, 1), 0.0))))
        pack_s[pl.ds(s0, CHUNK), :] = pack
        return carry

    prepass(0, 0)
    prepass(nc - 1, 0)

    def expand_matrix(row0):
        ej = lax.broadcasted_iota(jnp.int32, (CHUNK, gw), 0)
        ec = lax.broadcasted_iota(jnp.int32, (CHUNK, gw), 1)
        blocks = []
        for f in (PACK_DT, PACK_STATE, PACK_OFF):
            lo = (ej - f - row0) * P
            blocks.append(((ec >= lo) & (ec < lo + P)).astype(BF16))
        return jnp.concatenate(blocks, axis=1)

    expand_f = expand_matrix(0)
    expand_b = expand_matrix(BWD_ROW0)
    mask_f = ci <= ri
    mask_b = ci >= ri
    stf_s[...] = jnp.zeros_like(stf_s)
    stb_s[...] = jnp.zeros_like(stb_s)

    def scan_chunk(c, bwd):
        row0 = BWD_ROW0 if bwd else 0
        st_ref = stb_s if bwd else stf_s
        s0 = pl.multiple_of(c * CHUNK, CHUNK)
        pk = pack_s[pl.ds(s0, CHUNK), :]
        ex = jnp.dot(pk.astype(BF16), expand_b if bwd else expand_f, preferred_element_type=F32)
        dt_x = ex[:, :gw]
        fs_x = ex[:, gw:2 * gw]
        fo_x = ex[:, 2 * gw:]
        xc = xs_s[pl.ds(s0, CHUNK), :]
        x_dt = xc * dt_x.astype(BF16)
        x_st = xc * fs_x.astype(BF16)
        cc = c_s[pl.ds(s0, CHUNK), :]
        cbm = jnp.where(mask_b if bwd else mask_f, cb_s[pl.ds(s0, CHUNK), :].astype(F32), 0.0)
        state = st_ref[...]
        y = jnp.dot(cc, state.astype(BF16), preferred_element_type=F32) * fo_x
        er = erow_s[:, pl.ds(s0, CHUNK)]
        per_tile = LANES // P
        parts = []
        for q in range(gw // LANES):
            tile = x_dt[:, q * LANES:(q + 1) * LANES]
            acc = None
            for r in range(per_tile):
                j = row0 + q * per_tile + r
                col = pk[:, PACK_E + j:PACK_E + j + 1]
                row = er[j:j + 1, :]
                seg = (row - col) if bwd else (col - row)
                lm = (jnp.exp2(jnp.minimum(seg, 0.0)) * cbm).astype(BF16)
                rhs = jnp.where((ci >= r * P) & (ci < (r + 1) * P), tile, jnp.zeros_like(tile))
                d = jnp.dot(lm, rhs, preferred_element_type=F32)
                acc = d if acc is None else acc + d
            parts.append(acc)
        y = y + jnp.concatenate(parts, axis=1)
        e_total = fo_x[0:1, :] if bwd else fo_x[CHUNK - 1:CHUNK, :]
        st_ref[...] = state * e_total + jnp.dot(bt_s[:, pl.ds(s0, CHUNK)], x_st, preferred_element_type=F32)
        return y, xc

    def finish(y, xc, c):
        s0 = pl.multiple_of(c * CHUNK, CHUNK)
        v = (y + xc.astype(F32) * dsk_ref[...]) * _silu(z_ref[pl.ds(s0, CHUNK), :].astype(F32))
        ms = jnp.mean(v * v, axis=-1, keepdims=True)
        o_ref[pl.ds(s0, CHUNK), :] = (v * lax.rsqrt(ms + RMS_EPS) * gnw_ref[...]).astype(o_ref.dtype)

    def first_half(i, carry, prepare_next=True):
        cf = i
        cb_ = nc - 1 - i
        yf, _ = scan_chunk(cf, False)
        yb, _ = scan_chunk(cb_, True)
        o_ref[pl.ds(pl.multiple_of(cf * CHUNK, CHUNK), CHUNK), :] = yf.astype(o_ref.dtype)
        yb_s[pl.ds(pl.multiple_of((cb_ - half) * CHUNK, CHUNK), CHUNK), :] = yb.astype(yb_s.dtype)
        if prepare_next:
            prepass(cf + 1, 0)
            prepass(cb_ - 1, 0)
        return carry

    def second_half(i, carry):
        cf = i
        cb_ = nc - 1 - i
        yf, xf = scan_chunk(cf, False)
        yb, xb = scan_chunk(cb_, True)
        yb_prev = yb_s[pl.ds(pl.multiple_of((cf - half) * CHUNK, CHUNK), CHUNK), :].astype(F32)
        yf_prev = o_ref[pl.ds(pl.multiple_of(cb_ * CHUNK, CHUNK), CHUNK), :].astype(F32)
        finish(yf + yb_prev, xf, cf)
        finish(yb + yf_prev, xb, cb_)
        return carry

    lax.fori_loop(0, half - 1, first_half, 0)
    first_half(half - 1, 0, prepare_next=False)
    lax.fori_loop(half, nc, second_half, 0, unroll=4)


def _ssd(proj3, dt4, conv_w, conv_b, dt_bias_c, a_log_c, a_log_l, d_skip_x, gnorm_w, DI, P):
    B, L, _ = proj3.shape
    G = SSD_GROUPS
    gw = DI // G
    assert D_STATE == LANES and L % (2 * CHUNK) == 0 and gw % LANES == 0 and LANES % P == 0 and gw // P <= BWD_ROW0
    xoff = DI // gw
    boff = 2 * DI // LANES
    coff = boff + G
    cwb_off = DI // LANES
    seq_bytes = L * 2
    scratch_bytes = seq_bytes * (gw + 3 * D_STATE) + L * LANES * 4 + (L // 2) * gw * 2
    conv_in_bytes = seq_bytes * (gw + 2 * D_STATE)
    late_bytes = seq_bytes * 2 * gw
    budget = (VMEM_LIMIT_BYTES * 31) // 32
    single_late = 2 * (conv_in_bytes + late_bytes) + scratch_bytes > (VMEM_LIMIT_BYTES * 3) // 4
    single_conv = single_late and 2 * conv_in_bytes + late_bytes + scratch_bytes > budget

    def spec(shape, imap, single):
        if single:
            return pl.BlockSpec(shape, imap, pipeline_mode=pl.Buffered(1))
        return pl.BlockSpec(shape, imap)

    def big(shape, imap):
        return spec(shape, imap, single_conv)

    def late(shape, imap):
        return spec(shape, imap, single_late)

    in_specs = [
        late((None, L, gw), lambda b, g: (b, 0, g)),
        big((None, L, gw), lambda b, g: (b, 0, xoff + g)),
        big((None, L, D_STATE), lambda b, g: (b, 0, boff + g)),
        big((None, L, D_STATE), lambda b, g: (b, 0, coff + g)),
        pl.BlockSpec((None, None, DT_ROWS, L), lambda b, g: (b, g, 0, 0)),
        pl.BlockSpec((CONV_K, gw), lambda b, g: (0, g)),
        pl.BlockSpec((CONV_K, D_STATE), lambda b, g: (0, cwb_off + g)),
        pl.BlockSpec((CONV_K, D_STATE), lambda b, g: (0, cwb_off + G + g)),
        pl.BlockSpec((1, gw), lambda b, g: (0, g)),
        pl.BlockSpec((1, D_STATE), lambda b, g: (0, cwb_off + g)),
        pl.BlockSpec((1, D_STATE), lambda b, g: (0, cwb_off + G + g)),
        pl.BlockSpec((None, DT_ROWS, 1), lambda b, g: (g, 0, 0)),
        pl.BlockSpec((None, DT_ROWS, 1), lambda b, g: (g, 0, 0)),
        pl.BlockSpec((None, 1, LANES), lambda b, g: (g, 0, 0)),
        pl.BlockSpec((1, gw), lambda b, g: (0, g)),
        pl.BlockSpec((1, gw), lambda b, g: (0, g)),
    ]
    scratch = [
        pltpu.VMEM((L, gw), BF16),
        pltpu.VMEM((L, D_STATE), BF16),
        pltpu.VMEM((D_STATE, L), BF16),
        pltpu.VMEM((L, CHUNK), BF16),
        pltpu.VMEM((L, LANES), F32),
        pltpu.VMEM((DT_ROWS, L), F32),
        pltpu.VMEM((L // 2, gw), BF16),
        pltpu.VMEM((D_STATE, gw), F32),
        pltpu.VMEM((D_STATE, gw), F32),
    ]
    return pl.pallas_call(
        functools.partial(_ssd_kernel, L=L, P=P),
        grid=(B, G),
        in_specs=in_specs,
        out_specs=late((None, L, gw), lambda b, g: (b, 0, g)),
        out_shape=jax.ShapeDtypeStruct((B, L, DI), BF16),
        scratch_shapes=scratch,
        compiler_params=_cparams("parallel", "parallel"),
        name="ssd",
    )(proj3, proj3, proj3, proj3, dt4, conv_w, conv_w, conv_w, conv_b, conv_b, conv_b,
      dt_bias_c, a_log_c, a_log_l, d_skip_x, gnorm_w)


def _fft1_kernel(w_ref, tw_ref, u_ref, o_ref, *, L1, nl2, C):
    res = jnp.dot(w_ref[...], u_ref[...], preferred_element_type=F32)
    rep = C // LANES
    for q in range(nl2):
        a = res[:L1, q * C:(q + 1) * C]
        b = res[L1:, q * C:(q + 1) * C]
        twr = jnp.concatenate([tw_ref[0, q]] * rep, axis=1)
        twi = jnp.concatenate([tw_ref[1, q]] * rep, axis=1)
        o_ref[0, q] = (a * twr - b * twi).astype(o_ref.dtype)
        o_ref[1, q] = (a * twi + b * twr).astype(o_ref.dtype)


def _fft_stage1(u3, L1):
    B, L, C = u3.shape
    L2 = L // L1
    nl2 = 8
    k1 = np.arange(L1)
    ang1 = 2.0 * np.pi * np.outer(k1, np.arange(L1)) / L1
    w1 = jnp.asarray(np.concatenate([np.cos(ang1), -np.sin(ang1)], axis=0), F32).astype(BF16)
    angt = 2.0 * np.pi * np.outer(np.arange(L2), k1) / L
    tw = np.stack([np.cos(angt), -np.sin(angt)], axis=0)[..., None]
    tw = jnp.asarray(np.broadcast_to(tw, (2, L2, L1, LANES)), F32)
    uv = u3.reshape(B, L1, L2 * C)
    return pl.pallas_call(
        functools.partial(_fft1_kernel, L1=L1, nl2=nl2, C=C),
        grid=(B, L2 // nl2),
        in_specs=[pl.BlockSpec((2 * L1, L1), lambda b, j: (0, 0)),
                  pl.BlockSpec((2, nl2, L1, LANES), lambda b, j: (0, j, 0, 0)),
                  pl.BlockSpec((None, L1, nl2 * C), lambda b, j: (b, 0, j))],
        out_specs=pl.BlockSpec((None, 2, nl2, L1, C), lambda b, j: (b, 0, j, 0, 0)),
        out_shape=jax.ShapeDtypeStruct((B, 2, L2, L1, C), BF16),
        compiler_params=_cparams("parallel", "parallel"),
        name="fft_stage1",
    )(w1, tw, uv)


def _fft_stage2(g5):
    B, _, L2, L1, C = g5.shape
    ang = 2.0 * np.pi * np.outer(np.arange(L2), np.arange(L2)) / L2
    cs, sn = np.cos(ang), np.sin(ang)
    w2 = jnp.asarray(np.block([[cs, sn], [-sn, cs]]), F32).astype(BF16)
    N = L1 * C
    tn = _tile(N, FFT2_COLS, LANES)
    gv = g5.reshape(B, 2 * L2, N)
    y = pl.pallas_call(
        _mm_kernel,
        grid=(B, N // tn),
        in_specs=[pl.BlockSpec((2 * L2, 2 * L2), lambda b, j: (0, 0)),
                  pl.BlockSpec((None, 2 * L2, tn), lambda b, j: (b, 0, j))],
        out_specs=pl.BlockSpec((None, 2 * L2, tn), lambda b, j: (b, 0, j)),
        out_shape=jax.ShapeDtypeStruct((B, 2 * L2, N), BF16),
        compiler_params=_cparams("parallel", "parallel"),
        name="fft_stage2",
    )(w2, gv)
    return y.reshape(B, 2, L2 * L1, C)


def _four_out_kernel(y_ref, cs_ref, wf_ref, g_ref, t_ref, o_ref, f_s, *, scale):
    @pl.when(pl.program_id(2) == 0)
    def _():
        yr = y_ref[0]
        yi = y_ref[1]
        cs = cs_ref[...]
        parts = []
        for g in range(FOUR_GROUPS):
            sl = slice(g * FOUR_GROUP_DIM, (g + 1) * FOUR_GROUP_DIM)
            lhs = jnp.concatenate([yr[:, sl], yi[:, sl]], axis=1)
            parts.append(jnp.dot(lhs, cs, preferred_element_type=F32))
        f_s[...] = (jnp.concatenate(parts, axis=1) * scale).astype(f_s.dtype)

    yf = jnp.dot(f_s[...], wf_ref[...], preferred_element_type=F32)
    o_ref[...] = (t_ref[...].astype(F32) + _sigmoid(g_ref[...].astype(F32)) * yf).astype(o_ref.dtype)


def _four_out(y4, w_four, proj, gate_off, t1):
    B, _, L, C = y4.shape
    D = w_four.shape[1]
    tm = _tile(L, MM_ROWS)
    tn = _tile(D, FOUR_COLS, LANES, divides=(gate_off,))
    goff = gate_off // tn
    nti = L // tm
    ang = 2.0 * np.pi * np.outer(np.arange(FOUR_GROUP_DIM), np.arange(FOUR_GROUP_DIM)) / FOUR_GROUP_DIM
    cs = jnp.asarray(np.concatenate([np.cos(ang), np.sin(ang)], axis=0), F32).astype(BF16)
    scale = 1.0 / math.sqrt(L * FOUR_GROUP_DIM)
    return pl.pallas_call(
        functools.partial(_four_out_kernel, scale=scale),
        grid=(B, nti, D // tn),
        in_specs=[pl.BlockSpec((None, 2, tm, C), lambda b, i, j: (b, 0, i, 0)),
                  pl.BlockSpec((2 * FOUR_GROUP_DIM, FOUR_GROUP_DIM), lambda b, i, j: (0, 0)),
                  pl.BlockSpec((C, tn), lambda b, i, j: (0, j)),
                  pl.BlockSpec((tm, tn), lambda b, i, j: (b * nti + i, goff + j)),
                  pl.BlockSpec((tm, tn), lambda b, i, j: (b * nti + i, j))],
        out_specs=pl.BlockSpec((tm, tn), lambda b, i, j: (b * nti + i, j)),
        out_shape=jax.ShapeDtypeStruct((B * L, D), BF16),
        scratch_shapes=[pltpu.VMEM((tm, C), BF16)],
        compiler_params=_cparams("parallel", "parallel", "arbitrary"),
        name="four_out",
    )(y4, cs, w_four, proj, t1)


def _chan_dft_kernel(u_ref, cs_ref, o_ref):
    u = u_ref[...]
    cs = cs_ref[...]
    cos_parts, sin_parts = [], []
    for g in range(FOUR_GROUPS):
        r = jnp.dot(u[:, g * FOUR_GROUP_DIM:(g + 1) * FOUR_GROUP_DIM], cs, preferred_element_type=F32)
        cos_parts.append(r[:, :FOUR_GROUP_DIM])
        sin_parts.append(r[:, FOUR_GROUP_DIM:])
    o_ref[0] = jnp.concatenate(cos_parts, axis=1).astype(o_ref.dtype)
    o_ref[1] = jnp.concatenate(sin_parts, axis=1).astype(o_ref.dtype)


def _chan_dft(u, B, L):
    T, C = u.shape
    tm = _tile(L, MM_ROWS)
    nti = L // tm
    ang = 2.0 * np.pi * np.outer(np.arange(FOUR_GROUP_DIM), np.arange(FOUR_GROUP_DIM)) / FOUR_GROUP_DIM
    cs = jnp.asarray(np.concatenate([np.cos(ang), np.sin(ang)], axis=1), F32).astype(BF16)
    return pl.pallas_call(
        _chan_dft_kernel,
        grid=(B, nti),
        in_specs=[pl.BlockSpec((tm, C), lambda b, i: (b * nti + i, 0)),
                  pl.BlockSpec((FOUR_GROUP_DIM, 2 * FOUR_GROUP_DIM), lambda b, i: (0, 0))],
        out_specs=pl.BlockSpec((None, 2, tm, C), lambda b, i: (b, 0, i, 0)),
        out_shape=jax.ShapeDtypeStruct((B, 2, L, C), BF16),
        compiler_params=_cparams("parallel", "parallel"),
        name="chan_dft",
    )(u, cs)


def _dense_four_kernel(d_ref, z_ref, wf_ref, g_ref, t_ref, o_ref, *, scale):
    f = jnp.dot(d_ref[...], z_ref[...], preferred_element_type=F32)
    yf = jnp.dot((f * scale).astype(BF16), wf_ref[...], preferred_element_type=F32)
    o_ref[...] = (t_ref[...].astype(F32) + _sigmoid(g_ref[...].astype(F32)) * yf).astype(o_ref.dtype)


def _dense_four_out(z4, w_four, proj, gate_off, t1):
    B, _, L, C = z4.shape
    D = w_four.shape[1]
    tm = _tile(L, DFT_ROWS)
    nti = L // tm
    assert gate_off % D == 0
    goff = gate_off // D
    prod = (jnp.arange(L, dtype=jnp.int32)[:, None] * jnp.arange(L, dtype=jnp.int32)[None, :]) % L
    ang = prod.astype(F32) * (2.0 * math.pi / L)
    dmat = jnp.concatenate([jnp.cos(ang), -jnp.sin(ang)], axis=1).astype(BF16)
    scale = 1.0 / math.sqrt(L * FOUR_GROUP_DIM)
    return pl.pallas_call(
        functools.partial(_dense_four_kernel, scale=scale),
        grid=(B, nti),
        in_specs=[pl.BlockSpec((tm, 2 * L), lambda b, i: (i, 0)),
                  pl.BlockSpec((None, 2 * L, C), lambda b, i: (b, 0, 0)),
                  pl.BlockSpec((C, D), lambda b, i: (0, 0)),
                  pl.BlockSpec((tm, D), lambda b, i: (b * nti + i, goff)),
                  pl.BlockSpec((tm, D), lambda b, i: (b * nti + i, 0))],
        out_specs=pl.BlockSpec((tm, D), lambda b, i: (b * nti + i, 0)),
        out_shape=jax.ShapeDtypeStruct((B * L, D), BF16),
        compiler_params=_cparams("parallel", "arbitrary"),
        name="dense_four_out",
    )(dmat, z4.reshape(B, 2 * L, C), w_four, proj, t1)


def _ffn_kernel(te_ref, act_ref, x_ref, w1_ref, w3_ref, w2_ref, o_ref, acc_s):
    i = pl.program_id(0)
    j = pl.program_id(1)
    nj = pl.num_programs(1)

    @pl.when(j == 0)
    def _():
        acc_s[...] = jnp.zeros_like(acc_s)

    @pl.when(act_ref[i] > 0)
    def _():
        x = x_ref[...]
        h1 = jnp.dot(x, w1_ref[...], preferred_element_type=F32)
        h3 = jnp.dot(x, w3_ref[...], preferred_element_type=F32)
        h = (_silu(h1) * h3).astype(BF16)
        acc_s[...] += jnp.dot(h, w2_ref[...], preferred_element_type=F32)

    @pl.when(j == nj - 1)
    def _():
        o_ref[...] = acc_s[...].astype(o_ref.dtype)


def _ffn(x, w1, w3, w2, tile_expert, tile_active, tm):
    R, D = x.shape
    E, _, F = w1.shape
    tf = _tile(F, FFN_COLS, LANES)
    nf = F // tf

    def wcol(i, j, te, act):
        return (te[i], 0, jnp.where(act[i] > 0, j, nf - 1))

    def wrow(i, j, te, act):
        return (te[i], jnp.where(act[i] > 0, j, nf - 1), 0)

    grid_spec = pltpu.PrefetchScalarGridSpec(
        num_scalar_prefetch=2,
        grid=(R // tm, nf),
        in_specs=[pl.BlockSpec((tm, D), lambda i, j, te, act: (i, 0)),
                  pl.BlockSpec((None, D, tf), wcol),
                  pl.BlockSpec((None, D, tf), wcol),
                  pl.BlockSpec((None, tf, D), wrow)],
        out_specs=pl.BlockSpec((tm, D), lambda i, j, te, act: (i, 0)),
        scratch_shapes=[pltpu.VMEM((tm, D), F32)],
    )
    return pl.pallas_call(
        _ffn_kernel,
        grid_spec=grid_spec,
        out_shape=jax.ShapeDtypeStruct((R, D), BF16),
        compiler_params=_cparams("parallel", "arbitrary"),
        name="ffn",
    )(tile_expert, tile_active, x, w1, w3, w2)


def _router_kernel(x_ref, r_ref, w_ref, i_ref, *, E):
    xh, xm, _ = _split3(x_ref[...])
    rh, rm, _ = _split3(r_ref[...])
    logits = (jnp.dot(xh, rh, preferred_element_type=F32) + jnp.dot(xm, rh, preferred_element_type=F32)
              + jnp.dot(xh, rm, preferred_element_type=F32))
    lane = lax.broadcasted_iota(jnp.int32, logits.shape, 1).astype(F32)
    logits = jnp.where(lane < E, logits, -jnp.inf)
    ex = jnp.exp(logits - jnp.max(logits, axis=-1, keepdims=True))
    probs = ex / jnp.sum(ex, axis=-1, keepdims=True)
    m1 = jnp.max(probs, axis=-1, keepdims=True)
    i1 = jnp.min(jnp.where(probs == m1, lane, float(LANES)), axis=-1, keepdims=True)
    rest = jnp.where(lane == i1, -1.0, probs)
    m2 = jnp.max(rest, axis=-1, keepdims=True)
    i2 = jnp.min(jnp.where(rest == m2, lane, float(LANES)), axis=-1, keepdims=True)
    den = m1 + m2
    w_ref[...] = jnp.where(lane == 0.0, m1 / den, jnp.where(lane == 1.0, m2 / den, 0.0))
    i_ref[...] = jnp.where(lane == 0.0, i1, jnp.where(lane == 1.0, i2, 0.0)).astype(jnp.int32)


def _router(x, router):
    T, D = x.shape
    E = router.shape[1]
    tm = _tile(T, LN_ROWS)
    rp = jnp.zeros((D, LANES), F32).at[:, :E].set(router)
    row = pl.BlockSpec((tm, LANES), lambda i: (i, 0))
    return pl.pallas_call(
        functools.partial(_router_kernel, E=E),
        grid=(T // tm,),
        in_specs=[pl.BlockSpec((tm, D), lambda i: (i, 0)), pl.BlockSpec((D, LANES), lambda i: (0, 0))],
        out_specs=[row, row],
        out_shape=[jax.ShapeDtypeStruct((T, LANES), F32), jax.ShapeDtypeStruct((T, LANES), jnp.int32)],
        compiler_params=_cparams("parallel"),
        name="router",
    )(x, rp)


def _moe(xf, xb, router, we1, we3, we2, ln_g, ln_b):
    T, D = xf.shape
    E = we1.shape[0]
    tm = _tile(T, MM_ROWS)
    wt, idx = _router(xf, router)
    flat_e = idx[:, :TOP_K].reshape(-1)
    onehot = (flat_e[:, None] == jnp.arange(E, dtype=jnp.int32)[None, :]).astype(jnp.int32)
    rank = jnp.sum((jnp.cumsum(onehot, axis=0) - onehot) * onehot, axis=1)
    counts = jnp.sum(onehot, axis=0)
    padded = ((counts + tm - 1) // tm) * tm
    ends = jnp.cumsum(padded)
    pos = (ends - padded)[flat_e] + rank
    n_tiles = (TOP_K * T) // tm + E
    src = jnp.zeros((n_tiles * tm,), jnp.int32).at[pos].set(
        jnp.arange(TOP_K * T, dtype=jnp.int32) // TOP_K, unique_indices=True, mode="promise_in_bounds")
    starts = jnp.arange(n_tiles, dtype=jnp.int32) * tm
    tile_expert = jnp.minimum(jnp.searchsorted(ends, starts, side="right"), E - 1).astype(jnp.int32)
    tile_active = (starts < ends[-1]).astype(jnp.int32)
    x_sorted = xb.at[src].get(mode="promise_in_bounds")
    y_sorted = _ffn(x_sorted, we1, we3, we2, tile_expert, tile_active, tm)
    pos2 = pos.reshape(T, TOP_K)
    ya = y_sorted.at[pos2[:, 0]].get(mode="promise_in_bounds", unique_indices=True)
    yb = y_sorted.at[pos2[:, 1]].get(mode="promise_in_bounds", unique_indices=True)
    return _combine_ln(xf, ya, yb, wt, ln_g, ln_b)


def _prep_mixer(w_in, conv_w, conv_b, dt_bias_f, dt_bias_b, a_log_f, a_log_b, d_skip, gnorm_w,
                w_ssd_up, w_four, w_o):
    D = w_in.shape[0]
    DI = w_ssd_up.shape[0]
    CD = conv_b.shape[0]
    H = dt_bias_f.shape[0]
    DF = w_four.shape[0]
    G = SSD_GROUPS
    hpg = H // G
    o_dtf = DI + CD
    o_dtb = o_dtf + H
    o_u = o_dtb + H
    o_g = o_u + DF
    w_main = jnp.concatenate([w_in[:, :o_dtf], w_in[:, o_g:]], axis=1).astype(BF16)
    w_u = w_in[:, o_u:o_g].astype(BF16)

    def dt_rows(f, b, fill=0.0):
        lead = f.shape[:-1]
        out = jnp.full(lead + (G, DT_ROWS), fill, F32)
        out = out.at[..., :hpg].set(f.reshape(lead + (G, hpg)))
        return out.at[..., BWD_ROW0:BWD_ROW0 + hpg].set(b.reshape(lead + (G, hpg)))

    w_dt = dt_rows(w_in[:, o_dtf:o_dtb], w_in[:, o_dtb:o_u]).reshape(D, G * DT_ROWS).astype(BF16)
    dt_bias = dt_rows(dt_bias_f, dt_bias_b)
    a_log = dt_rows(a_log_f, a_log_b)
    a_log_l = jnp.zeros((G, 1, LANES), F32).at[:, 0, :DT_ROWS].set(a_log)
    return dict(
        w_main=w_main, w_udt=jnp.concatenate([w_u, w_dt], axis=1), w_u_cols=DF,
        conv_w=conv_w.reshape(CONV_K, CD), conv_b=conv_b.reshape(1, CD),
        dt_bias_c=dt_bias[..., None], a_log_c=a_log[..., None], a_log_l=a_log_l,
        d_skip_x=jnp.repeat(d_skip, DI // H).reshape(1, DI), gnorm_w=gnorm_w.reshape(1, DI),
        w_ssd_up=w_ssd_up.astype(BF16), w_four=w_four.astype(BF16), w_o=w_o.astype(BF16),
        DI=DI, P=DI // H, gate_off=o_dtf, D=D)


def _token_mixer(xf, xb, B, L, mp, ln_g, ln_b, after_proj=None):
    T = B * L
    G = SSD_GROUPS
    DI, D = mp["DI"], mp["D"]
    proj = _mm(xb, mp["w_main"], BF16)
    if after_proj is not None:
        proj, after_proj = lax.optimization_barrier((proj, after_proj))
    u, dt = _mm_split(xb, mp["w_udt"], mp["w_u_cols"], BF16, F32)
    dt4 = dt.reshape(B, L, G, DT_ROWS).transpose(0, 2, 3, 1)
    v = _ssd(proj.reshape(B, L, -1), dt4, mp["conv_w"], mp["conv_b"], mp["dt_bias_c"], mp["a_log_c"],
             mp["a_log_l"], mp["d_skip_x"], mp["gnorm_w"], DI, mp["P"])
    t1 = _gate_mm(v.reshape(T, DI), mp["w_ssd_up"], proj, mp["gate_off"])
    if L <= DENSE_DFT_MAX_L:
        merged = _dense_four_out(_chan_dft(u, B, L), mp["w_four"], proj, mp["gate_off"] + D, t1)
    else:
        g5 = _fft_stage1(u.reshape(B, L, -1), L // FFT_L2)
        y4 = _fft_stage2(g5)
        merged = _four_out(y4, mp["w_four"], proj, mp["gate_off"] + D, t1)
    xf_new, xb_new = _mm_res_ln(merged, mp["w_o"], xf, ln_g, ln_b)
    return xf_new, xb_new, after_proj


def kernel(x_prompt, x_sample, ln_in_g, ln_in_b, w_in_0, conv_w_0, conv_b_0, dt_bias_f_0, dt_bias_b_0, a_log_f_0, a_log_b_0, d_skip_0, gnorm_w_0, w_ssd_up_0, w_four_0, w_o_0, ln1_g_0, ln1_b_0, w1_0, w3_0, w2_0, ln2_g_0, ln2_b_0, w_in_1, conv_w_1, conv_b_1, dt_bias_f_1, dt_bias_b_1, a_log_f_1, a_log_b_1, d_skip_1, gnorm_w_1, w_ssd_up_1, w_four_1, w_o_1, ln1_g_1, ln1_b_1, router_1, we1_1, we3_1, we2_1, ln2_g_1, ln2_b_1):
    mp0 = _prep_mixer(w_in_0, conv_w_0, conv_b_0, dt_bias_f_0, dt_bias_b_0, a_log_f_0, a_log_b_0,
                      d_skip_0, gnorm_w_0, w_ssd_up_0, w_four_0, w_o_0)
    mp1 = _prep_mixer(w_in_1, conv_w_1, conv_b_1, dt_bias_f_1, dt_bias_b_1, a_log_f_1, a_log_b_1,
                      d_skip_1, gnorm_w_1, w_ssd_up_1, w_four_1, w_o_1)
    w1 = w1_0.astype(BF16)[None]
    w3 = w3_0.astype(BF16)[None]
    w2 = w2_0.astype(BF16)[None]
    we1, we3, we2, x_prompt = lax.optimization_barrier(
        (we1_1.astype(BF16), we3_1.astype(BF16), we2_1.astype(BF16), x_prompt))

    def mixers(x, after_proj=None):
        B, L, D = x.shape
        T = B * L
        xf, xb = _ln_in(x.reshape(T, D), ln_in_g, ln_in_b)
        xf, xb, _ = _token_mixer(xf, xb, B, L, mp0, ln1_g_0, ln1_b_0)
        tm = _tile(T, MM_ROWS)
        n_tiles = T // tm
        ffn = _ffn(xb, w1, w3, w2, jnp.zeros((n_tiles,), jnp.int32), jnp.ones((n_tiles,), jnp.int32), tm)
        xf, xb = _res_ln(xf, ffn, ln2_g_0, ln2_b_0)
        return _token_mixer(xf, xb, B, L, mp1, ln1_g_1, ln1_b_1, after_proj)

    def experts(xf, xb, shape):
        xf, _ = _moe(xf, xb, router_1, we1, we3, we2, ln2_g_1, ln2_b_1)
        return xf.reshape(shape)

    xf_s, xb_s, _ = mixers(x_sample)
    xf_p, xb_p, xb_s = mixers(x_prompt, after_proj=xb_s)
    return (experts(xf_p, xb_p, x_prompt.shape), experts(xf_s, xb_s, x_sample.shape))
```
